```python
import math
import numpy as np
import jax
import jax.numpy as jnp
from jax import lax

D_MODEL = 1024
BATCH = 1
SEQ = 16384
DEPTH = 2
DEC_BATCH = 32
DEC_SEQ = 8
PAST_LEN = 16384
PAGE_SIZE = 128

SB_HEADS = 8
SB_HEAD_DIM = 64
SB_WIDTH = SB_HEADS * SB_HEAD_DIM
SB_BLOCK = 128
GDN_HEADS = 4
GDN_HEAD_DIM = 128
GDN_WIDTH = GDN_HEADS * GDN_HEAD_DIM
GDN_CONV = 4
GDN_CHUNK = 64
IN0_WIDTH = 3 * SB_WIDTH + 3 * GDN_WIDTH + 2 * GDN_HEADS + GDN_WIDTH
MIX0_WIDTH = SB_WIDTH + GDN_WIDTH
RWKV_HEAD_DIM = 64
RWKV_HEADS = D_MODEL // RWKV_HEAD_DIM
RWKV_DECAY_LORA = 64
RWKV_AAA_LORA = 64
RWKV_GATE_LORA = 128
RWKV_GN_EPS = 64e-5
D_FF = 256 * ((8 * D_MODEL // 3 + 255) // 256)
N_EXPERTS = 8
TOP_K = 2
D_EXPERT = D_FF
DEEPNORM_ALPHA = (2 * DEPTH) ** 0.25
DEEPNORM_BETA = (8 * DEPTH) ** -0.25
LN_EPS = 1e-5
NORM_EPS = 1e-6

kernel_name = 'hybrid_sbattn_gdn_rwkv7_deepnorm_moe_step'


def layer_norm(x, g, b):
    xf = x.astype(jnp.float32)
    mu = jnp.mean(xf, -1, keepdims=True)
    var = jnp.mean(jnp.square(xf - mu), -1, keepdims=True)
    return ((xf - mu) * lax.rsqrt(var + LN_EPS) * g + b).astype(x.dtype)


def l2_normalize(x):
    xf = x.astype(jnp.float32)
    return xf * lax.rsqrt(jnp.sum(xf * xf, -1, keepdims=True) + NORM_EPS)


def rms_norm(x, w):
    xf = x.astype(jnp.float32)
    return xf * lax.rsqrt(jnp.mean(xf * xf, -1, keepdims=True) + NORM_EPS) * w


def swiglu(x, w_gate, w_up, w_down):
    return (jax.nn.silu(x @ w_gate) * (x @ w_up)) @ w_down


def _sb_block(q, k, v, q_pos, k_pos, bias):
    z = jnp.einsum('bqhd,bkhd->bhqk', q, k).astype(jnp.float32) * (SB_HEAD_DIM ** -0.5)
    z = z + bias.astype(jnp.float32)[None, :, None, None]
    visible = k_pos[None, :] < q_pos[:, None]
    log_keep = jnp.where(visible, jax.nn.log_sigmoid(-z), 0.0)
    log_after = lax.cumsum(log_keep, axis=3, reverse=True) - log_keep
    a = jnp.where(visible, jnp.exp(jax.nn.log_sigmoid(z) + log_after), 0.0)
    return jnp.einsum('bhqk,bkhd->bqhd', a.astype(v.dtype), v)


def stick_breaking_attention(q, k, v, past_len, bias):
    B, T, H, Dh = q.shape
    k_pos = jnp.arange(k.shape[1])
    q_pos = past_len + jnp.arange(T)
    if T <= SB_BLOCK:
        return _sb_block(q, k, v, q_pos, k_pos, bias)
    nb = -(-T // SB_BLOCK)
    pad = nb * SB_BLOCK - T
    qb = jnp.pad(q, ((0, 0), (0, pad), (0, 0), (0, 0))).reshape(B, nb, SB_BLOCK, H, Dh).swapaxes(0, 1)
    pb = jnp.pad(q_pos, (0, pad), constant_values=past_len + T - 1).reshape(nb, SB_BLOCK)
    out = lax.map(lambda blk: _sb_block(blk[0], k, v, blk[1], k_pos, bias), (qb, pb))
    return out.swapaxes(0, 1).reshape(B, nb * SB_BLOCK, H, Dh)[:, :T]


def gated_delta_chunked(q, k, v, g, beta, state0):
    B, T, H, _ = q.shape
    C = min(GDN_CHUNK, T)
    n = -(-T // C)
    pad = n * C - T

    def to_chunks(t):
        t = t.astype(jnp.float32)
        t = jnp.pad(t, [(0, 0), (0, pad)] + [(0, 0)] * (t.ndim - 2))
        t = t.reshape((B, n, C) + t.shape[2:])
        return jnp.transpose(t, (1, 0, 3, 2) + tuple(range(4, t.ndim)))

    qc, kc, vc, bc = to_chunks(q), to_chunks(k), to_chunks(v), to_chunks(beta)
    gc = jnp.cumsum(to_chunks(g), axis=-1)
    lower = jnp.tril(jnp.ones((C, C), bool))
    strict = jnp.tril(jnp.ones((C, C), bool), -1)
    decay = jnp.exp(jnp.where(lower, gc[..., :, None] - gc[..., None, :], -jnp.inf))
    kb = kc * bc[..., None]
    tri = jnp.where(strict, jnp.einsum('nbhid,nbhjd->nbhij', kb, kc) * decay, 0.0) + jnp.eye(C, dtype=jnp.float32)
    u = lax.linalg.triangular_solve(tri, vc * bc[..., None], left_side=True, lower=True, unit_diagonal=True)
    w = lax.linalg.triangular_solve(tri, kb * jnp.exp(gc)[..., None], left_side=True, lower=True, unit_diagonal=True)
    attn = jnp.einsum('nbhid,nbhjd->nbhij', qc, kc) * decay

    def step(S, inp):
        q_i, k_i, u_i, w_i, g_i, a_i = inp
        v_new = u_i - jnp.einsum('bhcd,bhde->bhce', w_i, S)
        o = jnp.einsum('bhcd,bhde->bhce', q_i * jnp.exp(g_i)[..., None], S) + jnp.einsum('bhij,bhje->bhie', a_i, v_new)
        g_last = g_i[..., -1:]
        S = S * jnp.exp(g_last)[..., None] + jnp.einsum('bhcd,bhce->bhde', k_i * jnp.exp(g_last - g_i)[..., None], v_new)
        return S, o

    S, o = lax.scan(step, state0.astype(jnp.float32), (qc, kc, u, w, gc, attn))
    o = jnp.transpose(o, (1, 0, 3, 2, 4)).reshape(B, n * C, H, v.shape[-1])[:, :T]
    return o, S


def even_mixer(x, past_k, past_v, conv_buf, gdn_state, w_in, sb_bias, conv_w, a_log, dt_bias, gdn_norm, w_out):
    B, T, _ = x.shape
    sizes = (SB_WIDTH, SB_WIDTH, SB_WIDTH, 3 * GDN_WIDTH, GDN_HEADS, GDN_HEADS, GDN_WIDTH)
    cuts = [int(c) for c in np.cumsum(sizes)[:-1]]
    sb_q, sb_k, sb_v, gdn_qkv, gdn_a, gdn_b, gdn_gate = jnp.split(x @ w_in, cuts, axis=-1)
    heads_sb = lambda t: t.reshape(B, T, SB_HEADS, SB_HEAD_DIM)
    k_new, v_new = heads_sb(sb_k), heads_sb(sb_v)
    k_all = jnp.concatenate([past_k.astype(x.dtype), k_new], axis=1)
    v_all = jnp.concatenate([past_v.astype(x.dtype), v_new], axis=1)
    o_sb = stick_breaking_attention(heads_sb(sb_q), k_all, v_all, past_k.shape[1], sb_bias).reshape(B, T, SB_WIDTH)
    seq = jnp.concatenate([conv_buf.astype(x.dtype), gdn_qkv], axis=1)
    conv = sum(seq[:, i:i + T] * conv_w[i] for i in range(GDN_CONV))
    new_conv_buf = seq[:, T:]
    gq, gk, gv = jnp.split(jax.nn.silu(conv), 3, axis=-1)
    heads_g = lambda t: t.reshape(B, T, GDN_HEADS, GDN_HEAD_DIM)
    gq = l2_normalize(heads_g(gq)) * (GDN_HEAD_DIM ** -0.5)
    gk = l2_normalize(heads_g(gk))
    beta = jax.nn.sigmoid(gdn_b.astype(jnp.float32))
    g = -jnp.exp(a_log.astype(jnp.float32)) * jax.nn.softplus(gdn_a.astype(jnp.float32) + dt_bias.astype(jnp.float32))
    o, new_state = gated_delta_chunked(gq, gk, heads_g(gv), g, beta, gdn_state)
    o = rms_norm(o, gdn_norm) * jax.nn.silu(heads_g(gdn_gate).astype(jnp.float32))
    o_gdn = o.reshape(B, T, GDN_WIDTH).astype(x.dtype)
    y = jnp.concatenate([o_sb, o_gdn], axis=-1) @ w_out
    return y, k_new, v_new, new_conv_buf, new_state.astype(gdn_state.dtype)


def rwkv7_scan(r, w, k, v, a, b, state0):
    def step(S, inp):
        r_t, w_t, k_t, v_t, a_t, b_t = inp
        sa = jnp.einsum('bhvk,bhk->bhv', S, a_t)
        S = S * w_t[:, :, None, :] + sa[..., None] * b_t[:, :, None, :] + v_t[..., None] * k_t[:, :, None, :]
        return S, jnp.einsum('bhvk,bhk->bhv', S, r_t)

    xs = tuple(jnp.moveaxis(t.astype(jnp.float32), 1, 0) for t in (r, w, k, v, a, b))
    S, o = lax.scan(step, state0.astype(jnp.float32), xs)
    return jnp.moveaxis(o, 0, 1), S


def rwkv7_mixer(x, shift_prev, state0, mix, w_r, w_k, w_v, w0, w1, w2, a0, a1, a2, g1, g2,
                k_k, k_a, r_k, lnx_g, lnx_b, w_o):
    B, T, D = x.shape
    heads = lambda t: t.reshape(B, T, RWKV_HEADS, RWKV_HEAD_DIM)
    x_prev = jnp.concatenate([shift_prev[:, None].astype(x.dtype), x[:, :-1]], axis=1)
    xx = x_prev - x
    xr, xw, xk, xv, xa, xg = (x + xx * mix[i] for i in range(6))
    r = xr @ w_r
    k = xk @ w_k
    v = xv @ w_v
    w_log = -jax.nn.softplus(-(w0 + jnp.tanh(xw @ w1) @ w2).astype(jnp.float32)) - 0.5
    decay = jnp.exp(-jnp.exp(w_log))
    a = jax.nn.sigmoid(a0 + (xa @ a1) @ a2)
    g = jax.nn.sigmoid(xg @ g1) @ g2
    kk = l2_normalize(heads(k * k_k))
    k = k * (1 + (a - 1) * k_a)
    r_h, k_h, v_h, a_h = heads(r), heads(k), heads(v), heads(a)
    o, state = rwkv7_scan(r_h, heads(decay), k_h, v_h, -kk, kk * a_h, state0)
    mu = jnp.mean(o, -1, keepdims=True)
    var = jnp.mean(jnp.square(o - mu), -1, keepdims=True)
    o = ((o - mu) * lax.rsqrt(var + RWKV_GN_EPS)).reshape(B, T, D) * lnx_g + lnx_b
    bonus = jnp.sum((r_h * k_h).astype(jnp.float32) * r_k, -1, keepdims=True) * v_h
    o = o + bonus.reshape(B, T, D)
    y = (o * g).astype(x.dtype) @ w_o
    return y, x[:, -1], state.astype(state0.dtype)


def moe_swiglu(x, w_router, w_gate, w_up, w_down):
    logits = jnp.einsum('btd,de->bte', x, w_router).astype(jnp.float32)
    top_logit, top_idx = lax.top_k(logits, TOP_K)
    top_w = jax.nn.softmax(top_logit, axis=-1)
    gate = jnp.einsum('btk,btke->bte', top_w, jax.nn.one_hot(top_idx, N_EXPERTS, dtype=jnp.float32)).astype(x.dtype)
    y = jnp.zeros_like(x)
    for e in range(N_EXPERTS):
        y = y + gate[..., e:e + 1] * swiglu(x, w_gate[e], w_up[e], w_down[e])
    return y


def trunk(x, past_k, past_v, gdn_conv, gdn_state, rwkv_shift, rwkv_state, p):
    for layer in range(DEPTH):
        if layer % 2 == 0:
            h, k_rows, v_rows, gdn_conv, gdn_state = even_mixer(
                x, past_k, past_v, gdn_conv, gdn_state, p['w_in0'], p['sb_bias'], p['gdn_conv_w'],
                p['gdn_a_log'], p['gdn_dt_bias'], p['gdn_norm_w'], p['w_out0'])
            x = layer_norm(DEEPNORM_ALPHA * x + h, p['ln_gamma'][layer, 0], p['ln_beta'][layer, 0])
            f = swiglu(x, p['ffn_gate'], p['ffn_up'], p['ffn_down'])
            x = layer_norm(DEEPNORM_ALPHA * x + f, p['ln_gamma'][layer, 1], p['ln_beta'][layer, 1])
        else:
            h, rwkv_shift, rwkv_state = rwkv7_mixer(
                x, rwkv_shift, rwkv_state, p['rwkv_mix'], p['rwkv_w_r'], p['rwkv_w_k'], p['rwkv_w_v'],
                p['rwkv_w0'], p['rwkv_w1'], p['rwkv_w2'], p['rwkv_a0'], p['rwkv_a1'], p['rwkv_a2'],
                p['rwkv_g1'], p['rwkv_g2'], p['rwkv_k_k'], p['rwkv_k_a'], p['rwkv_r_k'],
                p['rwkv_lnx_g'], p['rwkv_lnx_b'], p['rwkv_w_o'])
            x = layer_norm(DEEPNORM_ALPHA * x + h, p['ln_gamma'][layer, 0], p['ln_beta'][layer, 0])
            f = moe_swiglu(x, p['moe_router'], p['moe_gate'], p['moe_up'], p['moe_down'])
            x = layer_norm(DEEPNORM_ALPHA * x + f, p['ln_gamma'][layer, 1], p['ln_beta'][layer, 1])
    return x, k_rows, v_rows, gdn_conv, gdn_state, rwkv_shift, rwkv_state


def setup_inputs(seed: int = 0) -> dict:
    key = jax.random.key(seed)
    keys = iter(jax.random.split(key, 64))
    f32 = jnp.float32

    def normal(shape, scale):
        return jax.random.normal(next(keys), shape, f32) * scale

    def uniform(shape, lo, hi):
        return jax.random.uniform(next(keys), shape, f32, lo, hi)

    D = D_MODEL
    n_pages = PAST_LEN // PAGE_SIZE
    n_used = DEC_BATCH * n_pages
    n_pool = n_used + max(1, n_used // 4)
    x_prompt = normal((BATCH, SEQ, D), 1.0)
    x_sample = normal((DEC_BATCH, DEC_SEQ, D), 1.0)
    cache_k = normal((n_pool, PAGE_SIZE, SB_HEADS, SB_HEAD_DIM), 1.0)
    cache_v = normal((n_pool, PAGE_SIZE, SB_HEADS, SB_HEAD_DIM), 1.0)
    page_table = jax.random.permutation(next(keys), n_pool)[:n_used].reshape(DEC_BATCH, n_pages).astype(jnp.int32)
    state_gdn_conv = normal((DEC_BATCH, GDN_CONV - 1, 3 * GDN_WIDTH), 1.0)
    state_gdn = normal((DEC_BATCH, GDN_HEADS, GDN_HEAD_DIM, GDN_HEAD_DIM), 0.1)
    state_rwkv_shift = normal((DEC_BATCH, D), 1.0)
    state_rwkv = normal((DEC_BATCH, RWKV_HEADS, RWKV_HEAD_DIM, RWKV_HEAD_DIM), 0.1)
    w_in0 = normal((D, IN0_WIDTH), D ** -0.5)
    sb_bias = uniform((SB_HEADS,), -11.0, -5.0)
    gdn_conv_w = normal((GDN_CONV, 3 * GDN_WIDTH), GDN_CONV ** -0.5)
    gdn_a_log = jnp.log(uniform((GDN_HEADS,), 1.0, 16.0))
    dt = jnp.exp(uniform((GDN_HEADS,), math.log(1e-3), math.log(1e-1)))
    gdn_dt_bias = dt + jnp.log(-jnp.expm1(-dt))
    gdn_norm_w = 1.0 + normal((GDN_HEAD_DIM,), 0.02)
    w_out0 = normal((MIX0_WIDTH, D), MIX0_WIDTH ** -0.5 * DEEPNORM_BETA)
    ffn_gate = normal((D, D_FF), D ** -0.5)
    ffn_up = normal((D, D_FF), D ** -0.5)
    ffn_down = normal((D_FF, D), D_FF ** -0.5 * DEEPNORM_BETA)
    rwkv_mix = uniform((6, D), 0.0, 1.0)
    rwkv_w_r = normal((D, D), D ** -0.5)
    rwkv_w_k = normal((D, D), D ** -0.5)
    rwkv_w_v = normal((D, D), D ** -0.5)
    rwkv_w0 = uniform((D,), -6.0, -1.0)
    rwkv_w1 = normal((D, RWKV_DECAY_LORA), D ** -0.5)
    rwkv_w2 = normal((RWKV_DECAY_LORA, D), 0.5 * RWKV_DECAY_LORA ** -0.5)
    rwkv_a0 = normal((D,), 0.1)
    rwkv_a1 = normal((D, RWKV_AAA_LORA), D ** -0.5)
    rwkv_a2 = normal((RWKV_AAA_LORA, D), 0.5 * RWKV_AAA_LORA ** -0.5)
    rwkv_g1 = normal((D, RWKV_GATE_LORA), D ** -0.5)
    rwkv_g2 = normal((RWKV_GATE_LORA, D), RWKV_GATE_LORA ** -0.5)
    rwkv_k_k = 0.85 + normal((D,), 0.05)
    rwkv_k_a = 1.0 + normal((D,), 0.05)
    rwkv_r_k = normal((RWKV_HEADS, RWKV_HEAD_DIM), 0.1)
    rwkv_lnx_g = 1.0 + normal((D,), 0.02)
    rwkv_lnx_b = normal((D,), 0.02)
    rwkv_w_o = normal((D, D), D ** -0.5 * DEEPNORM_BETA)
    moe_router = normal((D, N_EXPERTS), D ** -0.5)
    moe_gate = normal((N_EXPERTS, D, D_EXPERT), D ** -0.5)
    moe_up = normal((N_EXPERTS, D, D_EXPERT), D ** -0.5)
    moe_down = normal((N_EXPERTS, D_EXPERT, D), D_EXPERT ** -0.5 * DEEPNORM_BETA)
    ln_gamma = 1.0 + normal((DEPTH, 2, D), 0.02)
    ln_beta = normal((DEPTH, 2, D), 0.02)
    return {'x_prompt': x_prompt, 'x_sample': x_sample, 'cache_k': cache_k, 'cache_v': cache_v,
            'page_table': page_table, 'state_gdn_conv': state_gdn_conv, 'state_gdn': state_gdn,
            'state_rwkv_shift': state_rwkv_shift, 'state_rwkv': state_rwkv,
            'w_in0': w_in0, 'sb_bias': sb_bias, 'gdn_conv_w': gdn_conv_w, 'gdn_a_log': gdn_a_log,
            'gdn_dt_bias': gdn_dt_bias,
            'gdn_norm_w': gdn_norm_w, 'w_out0': w_out0, 'ffn_gate': ffn_gate, 'ffn_up': ffn_up,
            'ffn_down': ffn_down, 'rwkv_mix': rwkv_mix, 'rwkv_w_r': rwkv_w_r, 'rwkv_w_k': rwkv_w_k,
            'rwkv_w_v': rwkv_w_v, 'rwkv_w0': rwkv_w0, 'rwkv_w1': rwkv_w1, 'rwkv_w2': rwkv_w2,
            'rwkv_a0': rwkv_a0, 'rwkv_a1': rwkv_a1, 'rwkv_a2': rwkv_a2, 'rwkv_g1': rwkv_g1,
            'rwkv_g2': rwkv_g2, 'rwkv_k_k': rwkv_k_k, 'rwkv_k_a': rwkv_k_a, 'rwkv_r_k': rwkv_r_k,
            'rwkv_lnx_g': rwkv_lnx_g, 'rwkv_lnx_b': rwkv_lnx_b, 'rwkv_w_o': rwkv_w_o,
            'moe_router': moe_router, 'moe_gate': moe_gate, 'moe_up': moe_up, 'moe_down': moe_down,
            'ln_gamma': ln_gamma, 'ln_beta': ln_beta}


def reference(x_prompt, x_sample, cache_k, cache_v, page_table, state_gdn_conv, state_gdn,
              state_rwkv_shift, state_rwkv, w_in0, sb_bias, gdn_conv_w, gdn_a_log, gdn_dt_bias, gdn_norm_w,
              w_out0, ffn_gate, ffn_up, ffn_down, rwkv_mix, rwkv_w_r, rwkv_w_k, rwkv_w_v, rwkv_w0,
              rwkv_w1, rwkv_w2, rwkv_a0, rwkv_a1, rwkv_a2, rwkv_g1, rwkv_g2, rwkv_k_k, rwkv_k_a,
              rwkv_r_k, rwkv_lnx_g, rwkv_lnx_b, rwkv_w_o, moe_router, moe_gate, moe_up, moe_down,
              ln_gamma, ln_beta):
    p = dict(w_in0=w_in0, sb_bias=sb_bias, gdn_conv_w=gdn_conv_w, gdn_a_log=gdn_a_log, gdn_dt_bias=gdn_dt_bias,
             gdn_norm_w=gdn_norm_w, w_out0=w_out0, ffn_gate=ffn_gate, ffn_up=ffn_up, ffn_down=ffn_down,
             rwkv_mix=rwkv_mix, rwkv_w_r=rwkv_w_r, rwkv_w_k=rwkv_w_k, rwkv_w_v=rwkv_w_v,
             rwkv_w0=rwkv_w0, rwkv_w1=rwkv_w1, rwkv_w2=rwkv_w2, rwkv_a0=rwkv_a0, rwkv_a1=rwkv_a1,
             rwkv_a2=rwkv_a2, rwkv_g1=rwkv_g1, rwkv_g2=rwkv_g2, rwkv_k_k=rwkv_k_k, rwkv_k_a=rwkv_k_a,
             rwkv_r_k=rwkv_r_k, rwkv_lnx_g=rwkv_lnx_g, rwkv_lnx_b=rwkv_lnx_b, rwkv_w_o=rwkv_w_o,
             moe_router=moe_router, moe_gate=moe_gate, moe_up=moe_up, moe_down=moe_down,
             ln_gamma=ln_gamma, ln_beta=ln_beta)
    bp, dt = x_prompt.shape[0], x_prompt.dtype
    empty = jnp.zeros((bp, 0, SB_HEADS, SB_HEAD_DIM), dt)
    (y_prompt, k_prompt, v_prompt, gdn_conv_prompt, gdn_state_prompt,
     rwkv_shift_prompt, rwkv_state_prompt) = trunk(
        x_prompt, empty, empty,
        jnp.zeros((bp, GDN_CONV - 1, 3 * GDN_WIDTH), dt),
        jnp.zeros((bp, GDN_HEADS, GDN_HEAD_DIM, GDN_HEAD_DIM), dt),
        jnp.zeros((bp, D_MODEL), dt),
        jnp.zeros((bp, RWKV_HEADS, RWKV_HEAD_DIM, RWKV_HEAD_DIM), dt), p)
    bs, n_pages = page_table.shape
    past_k = cache_k[page_table].reshape(bs, n_pages * PAGE_SIZE, SB_HEADS, SB_HEAD_DIM)
    past_v = cache_v[page_table].reshape(bs, n_pages * PAGE_SIZE, SB_HEADS, SB_HEAD_DIM)
    (y_sample, k_sample, v_sample, gdn_conv_sample, gdn_state_sample,
     rwkv_shift_sample, rwkv_state_sample) = trunk(
        x_sample, past_k, past_v, state_gdn_conv, state_gdn, state_rwkv_shift, state_rwkv, p)
    return (y_prompt, y_sample, k_prompt, v_prompt, gdn_conv_prompt, gdn_state_prompt,
            rwkv_shift_prompt, rwkv_state_prompt, k_sample, v_sample, gdn_conv_sample,
            gdn_state_sample, rwkv_shift_sample, rwkv_state_sample)
```

```python
import functools
import math

import jax
import jax.numpy as jnp
import numpy as np
from jax import lax
from jax.experimental import pallas as pl
from jax.experimental.pallas import tpu as pltpu

F32 = jnp.float32
BF16 = jnp.bfloat16

D_MODEL = 1024
SB_HEADS = 8
SB_HEAD_DIM = 64
SB_WIDTH = SB_HEADS * SB_HEAD_DIM
GDN_HEADS = 4
GDN_HEAD_DIM = 128
GDN_WIDTH = GDN_HEADS * GDN_HEAD_DIM
GDN_CONV = 4
RWKV_HEAD_DIM = 64
RWKV_HEADS = D_MODEL // RWKV_HEAD_DIM
RWKV_GN_EPS = 64e-5
N_EXPERTS = 8
DEPTH = 2
DEEPNORM_ALPHA = (2 * DEPTH) ** 0.25
LN_EPS = 1e-5
NORM_EPS = 1e-6

IN0_COLS = 4096
COL_BLOCK = 512
VMEM_LIMIT_BYTES = 56 * 1024 * 1024

_NT = (((1,), (1,)), ((), ()))
_TN = (((0,), (0,)), ((), ()))


def _params(*sem):
    return pltpu.CompilerParams(dimension_semantics=sem, vmem_limit_bytes=VMEM_LIMIT_BYTES)


def _dot(a, b):
    return jnp.dot(a.astype(BF16), b.astype(BF16), preferred_element_type=F32)


def _dot_nt(a, b):
    return lax.dot_general(a.astype(BF16), b.astype(BF16), _NT, preferred_element_type=F32)


def _dot_tn(a, b):
    return lax.dot_general(a.astype(BF16), b.astype(BF16), _TN, preferred_element_type=F32)


def _dot_exact_lhs(a01, x):
    a = a01.astype(BF16)
    x1 = x.astype(BF16)
    r1 = x - x1.astype(F32)
    x2 = r1.astype(BF16)
    x3 = (r1 - x2.astype(F32)).astype(BF16)
    out = jnp.dot(a, x1, preferred_element_type=F32)
    out = out + jnp.dot(a, x2, preferred_element_type=F32)
    return out + jnp.dot(a, x3, preferred_element_type=F32)


def _softplus(z):
    return jnp.maximum(z, 0.0) + jnp.log1p(jnp.exp(-jnp.abs(z)))


def _sigmoid(z):
    return 1.0 / (1.0 + jnp.exp(-z))


def _silu(z):
    return z * _sigmoid(z)


def _layer_norm(x, g, b):
    mu = jnp.mean(x, axis=-1, keepdims=True)
    xc = x - mu
    var = jnp.mean(xc * xc, axis=-1, keepdims=True)
    return xc * lax.rsqrt(var + LN_EPS) * g + b


def _inv_i_minus(n, size, block):
    rows = lax.broadcasted_iota(jnp.int32, (size, size), 0)
    cols = lax.broadcasted_iota(jnp.int32, (size, size), 1)
    p = jnp.where(rows == cols, 1.0, 0.0) + n
    y = n
    for _ in range(max(0, int(math.ceil(math.log2(block))) - 1)):
        y = _dot(y, y)
        p = p + _dot(p, y)
    return p


def _matmul_kernel(x_ref, w_ref, o_ref):
    o_ref[...] = jnp.dot(x_ref[...].astype(BF16), w_ref[...], preferred_element_type=F32)


def _matmul(x, w, tm, tn):
    n, k = x.shape
    m = w.shape[1]
    return pl.pallas_call(
        _matmul_kernel,
        grid=(n // tm, m // tn),
        in_specs=[pl.BlockSpec((tm, k), lambda i, j: (i, 0)),
                  pl.BlockSpec((k, tn), lambda i, j: (0, j))],
        out_specs=pl.BlockSpec((tm, tn), lambda i, j: (i, j)),
        out_shape=jax.ShapeDtypeStruct((n, m), F32),
        compiler_params=_params("parallel", "parallel"),
        name="in_proj",
    )(x, w)


def _proj_ln_kernel(*refs, n_in):
    a_refs = refs[:n_in]
    w_refs = refs[n_in:2 * n_in]
    x_ref, g_ref, b_ref, o_ref = refs[2 * n_in:]
    h = jnp.dot(a_refs[0][...], w_refs[0][...], preferred_element_type=F32)
    for a_ref, w_ref in zip(a_refs[1:], w_refs[1:]):
        h = h + jnp.dot(a_ref[...], w_ref[...], preferred_element_type=F32)
    o_ref[...] = _layer_norm(DEEPNORM_ALPHA * x_ref[...] + h, g_ref[...], b_ref[...])


def _proj_ln(acts, weights, x, gamma, beta, tm, name):
    n, d = x.shape
    n_in = len(acts)
    in_specs = [pl.BlockSpec((tm, a.shape[1]), lambda i: (i, 0)) for a in acts]
    in_specs += [pl.BlockSpec(w.shape, lambda i: (0, 0)) for w in weights]
    in_specs += [pl.BlockSpec((tm, d), lambda i: (i, 0)),
                 pl.BlockSpec((1, d), lambda i: (0, 0)),
                 pl.BlockSpec((1, d), lambda i: (0, 0))]
    return pl.pallas_call(
        functools.partial(_proj_ln_kernel, n_in=n_in),
        grid=(n // tm,),
        in_specs=in_specs,
        out_specs=pl.BlockSpec((tm, d), lambda i: (i, 0)),
        out_shape=jax.ShapeDtypeStruct((n, d), F32),
        compiler_params=_params("parallel"),
        name=name,
    )(*acts, *weights, x, gamma.reshape(1, d), beta.reshape(1, d))


def _top2_gates(logits):
    lane = lax.broadcasted_iota(jnp.int32, logits.shape, 1).astype(F32)
    big = float(logits.shape[1])
    lg = jnp.where(lane < N_EXPERTS, logits, -jnp.inf)
    m1 = jnp.max(lg, axis=-1, keepdims=True)
    i1 = jnp.min(jnp.where(lg == m1, lane, big), axis=-1, keepdims=True)
    lg2 = jnp.where(lane == i1, -jnp.inf, lg)
    m2 = jnp.max(lg2, axis=-1, keepdims=True)
    i2 = jnp.min(jnp.where(lg2 == m2, lane, big), axis=-1, keepdims=True)
    e2 = jnp.exp(m2 - m1)
    den = 1.0 + e2
    return jnp.where(lane == i1, 1.0 / den, 0.0) + jnp.where(lane == i2, e2 / den, 0.0)


def _ffn_ln_kernel(x_ref, wg_ref, wu_ref, wd_ref, g_ref, b_ref, *rest, routed):
    if routed:
        wr_ref, o_ref, acc_ref, gate_ref = rest
    else:
        o_ref, acc_ref = rest
    e = pl.program_id(1)
    f = pl.program_id(2)
    first = jnp.logical_and(e == 0, f == 0)
    last = jnp.logical_and(e == pl.num_programs(1) - 1, f == pl.num_programs(2) - 1)

    @pl.when(first)
    def _():
        acc_ref[...] = jnp.zeros_like(acc_ref)
        if routed:
            logits = jnp.dot(x_ref[...], wr_ref[...], preferred_element_type=F32,
                             precision=lax.Precision.HIGHEST)
            gate_ref[...] = _top2_gates(logits)

    xb = x_ref[...].astype(BF16)
    hg = jnp.dot(xb, wg_ref[0], preferred_element_type=F32)
    hu = jnp.dot(xb, wu_ref[0], preferred_element_type=F32)
    y = jnp.dot((_silu(hg) * hu).astype(BF16), wd_ref[0], preferred_element_type=F32)
    if routed:
        gates = gate_ref[...]
        lane = lax.broadcasted_iota(jnp.int32, gates.shape, 1)
        y = y * jnp.sum(jnp.where(lane == e, gates, 0.0), axis=-1, keepdims=True)
    acc_ref[...] += y

    @pl.when(last)
    def _():
        o_ref[...] = _layer_norm(DEEPNORM_ALPHA * x_ref[...] + acc_ref[...], g_ref[...], b_ref[...])


def _ffn_ln(x, w_gate, w_up, w_down, gamma, beta, w_router, tm, tf, name):
    n, d = x.shape
    n_e, _, d_ff = w_gate.shape
    routed = w_router is not None
    in_specs = [pl.BlockSpec((tm, d), lambda i, e, f: (i, 0)),
                pl.BlockSpec((1, d, tf), lambda i, e, f: (e, 0, f)),
                pl.BlockSpec((1, d, tf), lambda i, e, f: (e, 0, f)),
                pl.BlockSpec((1, tf, d), lambda i, e, f: (e, f, 0)),
                pl.BlockSpec((1, d), lambda i, e, f: (0, 0)),
                pl.BlockSpec((1, d), lambda i, e, f: (0, 0))]
    args = [x, w_gate, w_up, w_down, gamma.reshape(1, d), beta.reshape(1, d)]
    scratch = [pltpu.VMEM((tm, d), F32)]
    if routed:
        in_specs.append(pl.BlockSpec(w_router.shape, lambda i, e, f: (0, 0)))
        args.append(w_router)
        scratch.append(pltpu.VMEM((tm, w_router.shape[1]), F32))
    return pl.pallas_call(
        functools.partial(_ffn_ln_kernel, routed=routed),
        grid=(n // tm, n_e, d_ff // tf),
        in_specs=in_specs,
        out_specs=pl.BlockSpec((tm, d), lambda i, e, f: (i, 0)),
        out_shape=jax.ShapeDtypeStruct((n, d), F32),
        scratch_shapes=scratch,
        compiler_params=_params("parallel", "arbitrary", "arbitrary"),
        name=name,
    )(*args)


def _rwkv_proj_kernel(x_ref, xp_ref, mix_ref, wr_ref, wk_ref, wv_ref, w1_ref, w2_ref, a1_ref, a2_ref,
                      g1_ref, g2_ref, w0_ref, a0_ref, r_ref, k_ref, v_ref, lw_ref, a_ref, g_ref):
    x = x_ref[...]
    xx = xp_ref[...] - x
    mixed = lambda i: (x + xx * mix_ref[i:i + 1, :]).astype(BF16)
    r_ref[...] = jnp.dot(mixed(0), wr_ref[...], preferred_element_type=F32)
    w_lora = _dot(jnp.tanh(jnp.dot(mixed(1), w1_ref[...], preferred_element_type=F32)), w2_ref[...])
    lw_ref[...] = -jnp.exp(-_softplus(-(w0_ref[...] + w_lora)) - 0.5)
    k_ref[...] = jnp.dot(mixed(2), wk_ref[...], preferred_element_type=F32)
    v_ref[...] = jnp.dot(mixed(3), wv_ref[...], preferred_element_type=F32)
    a_lora = _dot(jnp.dot(mixed(4), a1_ref[...], preferred_element_type=F32), a2_ref[...])
    a_ref[...] = _sigmoid(a0_ref[...] + a_lora)
    g_ref[...] = _dot(_sigmoid(jnp.dot(mixed(5), g1_ref[...], preferred_element_type=F32)), g2_ref[...])


def _rwkv_proj(x, x_prev, mix, w_r, w_k, w_v, w1, w2, a1, a2, g1, g2, w0, a0, tm):
    n, d = x.shape
    tok = pl.BlockSpec((tm, d), lambda i: (i, 0))
    full = lambda a: pl.BlockSpec(a.shape, lambda i: (0, 0))
    consts = [mix, w_r, w_k, w_v, w1, w2, a1, a2, g1, g2, w0, a0]
    return pl.pallas_call(
        _rwkv_proj_kernel,
        grid=(n // tm,),
        in_specs=[tok, tok] + [full(c) for c in consts],
        out_specs=[tok] * 6,
        out_shape=[jax.ShapeDtypeStruct((n, d), F32)] * 6,
        compiler_params=_params("parallel"),
        name="rwkv_proj",
    )(x, x_prev, *consts)


def _suffix_sum_rows8(x):
    row = lax.broadcasted_iota(jnp.int32, x.shape, 0)
    for sh in (1, 2, 4):
        shifted = pltpu.roll(x, 8 - sh, axis=0)
        x = x + jnp.where(row < 8 - sh, shifted, 0.0)
    return x


def _sb_block(z, carry, vis, nsub):
    tk, tq = z.shape
    sp = _softplus(z)
    log_beta = z - sp
    if vis is not None:
        sp = jnp.where(vis, sp, 0.0)
    sp3 = sp.reshape(nsub, 8, tq)
    run = jnp.zeros((8, tq), F32)
    after = [None] * nsub
    for s in reversed(range(nsub)):
        after[s] = run
        run = run + sp3[s]
    incl = _suffix_sum_rows8(run)
    offset = incl - run + carry
    a = jnp.exp(log_beta.reshape(nsub, 8, tq) - jnp.stack(after, axis=0) - offset[None])
    a = a.reshape(tk, tq)
    if vis is not None:
        a = jnp.where(vis, a, 0.0)
    return a, carry + incl[0:1]


def _sb_prompt_kernel(bias_ref, q_ref, k_ref, vt_ref, o_ref, *, tq, tk):
    h = pl.program_id(0)
    i = pl.program_id(1)
    bias = bias_ref[h]
    q = q_ref[0]
    nsub = tk // 8
    ratio = tq // tk

    def step(j, acc, carry, masked):
        start = pl.multiple_of(j * tk, tk)
        k = k_ref[0, pl.ds(start, tk), :]
        z = lax.dot_general(k, q, _NT, preferred_element_type=F32) + bias
        vis = None
        if masked:
            row = lax.broadcasted_iota(jnp.int32, (tk, tq), 0)
            col = lax.broadcasted_iota(jnp.int32, (tk, tq), 1)
            k_pos = j * tk + (row % 8) * nsub + row // 8
            vis = k_pos < i * tq + col
        a, carry = _sb_block(z, carry, vis, nsub)
        vt = vt_ref[0, :, pl.ds(start, tk)]
        return acc + jnp.dot(vt, a.astype(BF16), preferred_element_type=F32), carry

    acc = jnp.zeros((SB_HEAD_DIM, tq), F32)
    carry = jnp.zeros((1, tq), F32)
    for d in range(ratio):
        acc, carry = step(i * ratio + (ratio - 1 - d), acc, carry, True)

    def body(t, state):
        return step(i * ratio - 1 - t, state[0], state[1], False)

    acc, carry = lax.fori_loop(0, i * ratio, body, (acc, carry))
    o_ref[0] = acc.astype(o_ref.dtype)


def _sb_prompt(q, k, v, bias, tq, tk):
    t = q.shape[0]
    nblk, nsub = t // tk, tk // 8
    qh = q.astype(BF16).reshape(t, SB_HEADS, SB_HEAD_DIM).transpose(1, 0, 2)
    k5 = k.astype(BF16).reshape(nblk, 8, nsub, SB_HEADS, SB_HEAD_DIM)
    kh = k5.transpose(3, 0, 2, 1, 4).reshape(SB_HEADS, t, SB_HEAD_DIM)
    v5 = v.astype(BF16).reshape(nblk, 8, nsub, SB_HEADS, SB_HEAD_DIM)
    vt = v5.transpose(3, 4, 0, 2, 1).reshape(SB_HEADS, SB_HEAD_DIM, t)
    out_t = pl.pallas_call(
        functools.partial(_sb_prompt_kernel, tq=tq, tk=tk),
        grid=(SB_HEADS, t // tq),
        in_specs=[pl.BlockSpec(memory_space=pltpu.SMEM),
                  pl.BlockSpec((1, tq, SB_HEAD_DIM), lambda h, i: (h, i, 0)),
                  pl.BlockSpec((1, t, SB_HEAD_DIM), lambda h, i: (h, 0, 0)),
                  pl.BlockSpec((1, SB_HEAD_DIM, t), lambda h, i: (h, 0, 0))],
        out_specs=pl.BlockSpec((1, SB_HEAD_DIM, tq), lambda h, i: (h, 0, i)),
        out_shape=jax.ShapeDtypeStruct((SB_HEADS, SB_HEAD_DIM, t), BF16),
        compiler_params=_params("parallel", "parallel"),
        name="sb_prompt",
    )(bias, qh, kh, vt)
    return out_t.transpose(2, 0, 1).reshape(t, SB_WIDTH)


def _sb_sample_kernel(pt_ref, qbd_ref, bias_ref, kn_ref, vn_ref, *rest, pages_per_step, n_q):
    del pt_ref
    k_refs = rest[:pages_per_step]
    v_refs = rest[pages_per_step:2 * pages_per_step]
    o_ref, acc_ref, carry_ref = rest[2 * pages_per_step:]
    g = pl.program_id(1)
    qbd = qbd_ref[0]
    bias = bias_ref[...]
    n_col = qbd.shape[0]

    @pl.when(g == 0)
    def _():
        z = _dot_nt(kn_ref[0], qbd) + bias
        row = lax.broadcasted_iota(jnp.int32, z.shape, 0)
        col = lax.broadcasted_iota(jnp.int32, z.shape, 1)
        vis = row < col % n_q
        sp = jnp.where(vis, _softplus(z), 0.0)
        incl = _suffix_sum_rows8(sp)
        a = jnp.where(vis, jnp.exp(z - _softplus(z) - (incl - sp)), 0.0)
        acc_ref[...] = _dot_tn(a, vn_ref[0])
        carry_ref[...] = incl[0:1]

    acc = acc_ref[...]
    carry = carry_ref[...]
    page = k_refs[0].shape[1]
    row = lax.broadcasted_iota(jnp.int32, (page, page), 0)
    col = lax.broadcasted_iota(jnp.int32, (page, page), 1)
    later = jnp.where(col > row, 1.0, 0.0)
    for k_ref, v_ref in zip(k_refs, v_refs):
        z = _dot_nt(k_ref[0], qbd) + bias
        sp = _softplus(z)
        after = _dot_exact_lhs(later, sp)
        a = jnp.exp(z - sp - after - carry)
        acc = acc + _dot_tn(a, v_ref[0])
        carry = carry + jnp.sum(sp, axis=0, keepdims=True)
    acc_ref[...] = acc
    carry_ref[...] = carry

    @pl.when(g == pl.num_programs(1) - 1)
    def _():
        r = lax.broadcasted_iota(jnp.int32, acc.shape, 0)
        c = lax.broadcasted_iota(jnp.int32, acc.shape, 1)
        own = jnp.where(r // n_q == c // SB_HEAD_DIM, acc, 0.0)
        o_ref[0] = jnp.sum(own.reshape(SB_HEADS, n_q, SB_WIDTH), axis=0).astype(o_ref.dtype)


def _sb_sample(q, k_new, v_new, cache_k, cache_v, page_table, bias, pages_per_step):
    b, n_q, _ = q.shape
    n_pages = page_table.shape[1]
    n_pool, page = cache_k.shape[:2]
    ck = cache_k.reshape(n_pool, page, SB_WIDTH)
    cv = cache_v.reshape(n_pool, page, SB_WIDTH)
    q4 = q.astype(BF16).reshape(b, n_q, SB_HEADS, SB_HEAD_DIM).transpose(0, 2, 1, 3)
    eye = jnp.eye(SB_HEADS, dtype=BF16)
    qbd = (q4[:, :, :, None, :] * eye[None, :, None, :, None]).reshape(b, SB_HEADS * n_q, SB_WIDTH)
    bias_row = jnp.repeat(bias.astype(F32), n_q).reshape(1, SB_HEADS * n_q)
    steps = n_pages // pages_per_step

    def page_map(u):
        return lambda s, g, pt: (pt[s, n_pages - 1 - (g * pages_per_step + u)], 0, 0)

    page_specs = [pl.BlockSpec((1, page, SB_WIDTH), page_map(u)) for u in range(pages_per_step)]
    per_seq = lambda shape: pl.BlockSpec((1,) + shape, lambda s, g, pt: (s, 0, 0))
    grid_spec = pltpu.PrefetchScalarGridSpec(
        num_scalar_prefetch=1,
        grid=(b, steps),
        in_specs=[per_seq((SB_HEADS * n_q, SB_WIDTH)),
                  pl.BlockSpec((1, SB_HEADS * n_q), lambda s, g, pt: (0, 0)),
                  per_seq((n_q, SB_WIDTH)), per_seq((n_q, SB_WIDTH))] + page_specs + page_specs,
        out_specs=per_seq((n_q, SB_WIDTH)),
        scratch_shapes=[pltpu.VMEM((SB_HEADS * n_q, SB_WIDTH), F32),
                        pltpu.VMEM((1, SB_HEADS * n_q), F32)],
    )
    return pl.pallas_call(
        functools.partial(_sb_sample_kernel, pages_per_step=pages_per_step, n_q=n_q),
        grid_spec=grid_spec,
        out_shape=jax.ShapeDtypeStruct((b, n_q, SB_WIDTH), BF16),
        compiler_params=_params("parallel", "arbitrary"),
        name="sb_sample",
    )(page_table, qbd, bias_row, k_new, v_new, *([ck] * pages_per_step), *([cv] * pages_per_step))


def _gdn_kernel(hp_ref, qkv_ref, gate_ref, ab_ref, cbuf_ref, cw_ref, nw_ref, s0_ref,
                o_ref, s_out_ref, s_ref, carry_ref, *, chunk):
    c = pl.program_id(1)

    @pl.when(c == 0)
    def _():
        s_ref[...] = s0_ref[0]
        carry_ref[...] = cbuf_ref[0]

    x = qkv_ref[...]
    ext = jnp.concatenate([carry_ref[...], x], axis=0)
    conv = ext[5:5 + chunk] * cw_ref[0:1, :]
    for i in range(1, GDN_CONV):
        conv = conv + ext[5 + i:5 + i + chunk] * cw_ref[i:i + 1, :]
    carry_ref[...] = ext[chunk:chunk + 8]
    act = _silu(conv)

    rows = lax.broadcasted_iota(jnp.int32, (chunk, chunk), 0)
    cols = lax.broadcasted_iota(jnp.int32, (chunk, chunk), 1)
    lower = cols <= rows
    strict = cols < rows
    lower01 = jnp.where(lower, 1.0, 0.0)
    ab = ab_ref[...]
    d = GDN_HEAD_DIM
    for h in range(GDN_HEADS):
        qh = act[:, h * d:(h + 1) * d]
        kh = act[:, GDN_WIDTH + h * d:GDN_WIDTH + (h + 1) * d]
        vh = act[:, 2 * GDN_WIDTH + h * d:2 * GDN_WIDTH + (h + 1) * d]
        qn = qh * lax.rsqrt(jnp.sum(qh * qh, axis=-1, keepdims=True) + NORM_EPS) * (d ** -0.5)
        kn = kh * lax.rsqrt(jnp.sum(kh * kh, axis=-1, keepdims=True) + NORM_EPS)
        beta = _sigmoid(ab[:, GDN_HEADS + h:GDN_HEADS + h + 1])
        rate = jnp.exp(hp_ref[0:1, h:h + 1])
        g = -rate * _softplus(ab[:, h:h + 1] + hp_ref[1:2, h:h + 1])
        gc = _dot_exact_lhs(lower01, jnp.broadcast_to(g, (chunk, d)))
        gi = gc[:, :chunk] if chunk <= d else jnp.broadcast_to(gc[:, :1], (chunk, chunk))
        decay = jnp.exp(jnp.where(lower, gi - gi.T, -jnp.inf))
        kb = kn * beta
        tri = jnp.where(strict, _dot_nt(kb, kn) * decay, 0.0)
        t_inv = _inv_i_minus(-tri, chunk, chunk)
        e_gc = jnp.exp(gc)
        uw = _dot(t_inv, jnp.concatenate([vh * beta, kb * e_gc], axis=-1))
        u, w = uw[:, :d], uw[:, d:]
        attn = jnp.where(lower, _dot_nt(qn, kn) * decay, 0.0)
        s = s_ref[h]
        v_new = u - _dot_nt(w, s)
        o = _dot_nt(qn * e_gc, s) + _dot(attn, v_new)
        g_last = gc[chunk - 1:chunk, :]
        s_ref[h] = s * jnp.exp(g_last) + _dot_tn(v_new, kn * jnp.exp(g_last - gc))
        o = o * lax.rsqrt(jnp.mean(o * o, axis=-1, keepdims=True) + NORM_EPS) * nw_ref[...]
        o_ref[:, h * d:(h + 1) * d] = (o * _silu(gate_ref[:, h * d:(h + 1) * d])).astype(o_ref.dtype)

    @pl.when(c == pl.num_programs(1) - 1)
    def _():
        s_out_ref[0] = s_ref[...]


def _gdn(y_all, row0, n_seq, seq_len, chunk, conv_buf, state0, conv_w, a_log, dt_bias, norm_w):
    n_chunks = seq_len // chunk
    blk0 = row0 // chunk
    row_map = lambda col: (lambda s, c: (blk0 + s * n_chunks + c, col))
    cbuf = jnp.pad(conv_buf, ((0, 0), (8 - (GDN_CONV - 1), 0), (0, 0)))
    cw = jnp.pad(conv_w, ((0, 8 - GDN_CONV), (0, 0)))
    head_params = jnp.zeros((8, 128), F32).at[0, :GDN_HEADS].set(a_log).at[1, :GDN_HEADS].set(dt_bias)
    state_spec = pl.BlockSpec((1, GDN_HEADS, GDN_HEAD_DIM, GDN_HEAD_DIM), lambda s, c: (s, 0, 0, 0))
    o, s_new = pl.pallas_call(
        functools.partial(_gdn_kernel, chunk=chunk),
        grid=(n_seq, n_chunks),
        in_specs=[pl.BlockSpec((8, 128), lambda s, c: (0, 0)),
                  pl.BlockSpec((chunk, 3 * GDN_WIDTH), row_map(1)),
                  pl.BlockSpec((chunk, COL_BLOCK), row_map(6)),
                  pl.BlockSpec((chunk, COL_BLOCK), row_map(7)),
                  pl.BlockSpec((1, 8, 3 * GDN_WIDTH), lambda s, c: (s, 0, 0)),
                  pl.BlockSpec((8, 3 * GDN_WIDTH), lambda s, c: (0, 0)),
                  pl.BlockSpec((1, GDN_HEAD_DIM), lambda s, c: (0, 0)),
                  state_spec],
        out_specs=[pl.BlockSpec((chunk, GDN_WIDTH), lambda s, c: (s * n_chunks + c, 0)), state_spec],
        out_shape=[jax.ShapeDtypeStruct((n_seq * seq_len, GDN_WIDTH), BF16),
                   jax.ShapeDtypeStruct(state0.shape, F32)],
        scratch_shapes=[pltpu.VMEM((GDN_HEADS, GDN_HEAD_DIM, GDN_HEAD_DIM), F32),
                        pltpu.VMEM((8, 3 * GDN_WIDTH), F32)],
        compiler_params=_params("parallel", "arbitrary"),
        name="gdn",
    )(head_params, y_all, y_all, y_all, cbuf, cw, norm_w.reshape(1, GDN_HEAD_DIM),
      jnp.swapaxes(state0, -1, -2))
    return o, jnp.swapaxes(s_new, -1, -2)


def _rwkv_kernel(r_ref, k_ref, v_ref, lw_ref, a_ref, g_ref, kk_ref, ka_ref, rk_ref, lg_ref, lb_ref,
                 s0_ref, o_ref, s_out_ref, s_ref, *, chunk, group):
    c = pl.program_id(1)

    @pl.when(c == 0)
    def _():
        s_ref[...] = s0_ref[0]

    hd = RWKV_HEAD_DIM
    width = group * hd
    size = group * chunk
    rows = lax.broadcasted_iota(jnp.int32, (size, size), 0)
    cols = lax.broadcasted_iota(jnp.int32, (size, size), 1)
    same = rows // chunk == cols // chunk
    lower = jnp.logical_and(same, cols <= rows)
    strict = jnp.logical_and(same, cols < rows)
    lower01 = jnp.where(lower, 1.0, 0.0)
    r2 = lax.broadcasted_iota(jnp.int32, (size, width), 0)
    c2 = lax.broadcasted_iota(jnp.int32, (size, width), 1)
    own = r2 // chunk == c2 // hd
    stack = lambda x: jnp.where(own, jnp.concatenate([x] * group, axis=0), 0.0)

    for gi in range(D_MODEL // width):
        sl = slice(gi * width, (gi + 1) * width)
        lw_n = lw_ref[:, sl]
        r = stack(r_ref[:, sl])
        k = stack(k_ref[:, sl])
        v = stack(v_ref[:, sl])
        a = stack(a_ref[:, sl])
        lw = stack(lw_n)
        kk = k * kk_ref[:, sl]
        kk = kk * lax.rsqrt(jnp.sum(kk * kk, axis=-1, keepdims=True) + NORM_EPS)
        k = k * (1.0 + (a - 1.0) * ka_ref[:, sl])
        b = kk * a
        cw = _dot_exact_lhs(lower01, lw)
        tot = jnp.sum(lw_n, axis=0, keepdims=True)
        e_neg = jnp.exp(-cw)
        e_end = jnp.exp(jnp.where(own, tot - cw, 0.0))
        at = -kk * jnp.exp(cw - lw)
        rt = r * jnp.exp(cw)
        bt, kt = b * e_neg, k * e_neg
        a_ab = jnp.where(strict, _dot_nt(at, bt), 0.0)
        a_ak = jnp.where(strict, _dot_nt(at, kt), 0.0)
        a_rb = jnp.where(lower, _dot_nt(rt, bt), 0.0)
        a_rk = jnp.where(lower, _dot_nt(rt, kt), 0.0)
        t_inv = _inv_i_minus(a_ab, size, chunk)
        s = s_ref[gi]
        u = _dot(t_inv, _dot_nt(at, s) + _dot(a_ak, v))
        o = _dot_nt(rt, s) + _dot(a_rb, u) + _dot(a_rk, v)
        s_ref[gi] = s * jnp.exp(tot) + _dot_tn(u, b * e_end) + _dot_tn(v, k * e_end)
        mu = jnp.sum(o, axis=-1, keepdims=True) * (1.0 / hd)
        oc = jnp.where(own, o - mu, 0.0)
        var = jnp.sum(oc * oc, axis=-1, keepdims=True) * (1.0 / hd)
        on = oc * lax.rsqrt(var + RWKV_GN_EPS) * lg_ref[:, sl] + jnp.where(own, lb_ref[:, sl], 0.0)
        on = on + jnp.sum(r * k * rk_ref[:, sl], axis=-1, keepdims=True) * v
        out = on[0:chunk]
        for hh in range(1, group):
            out = out + on[hh * chunk:(hh + 1) * chunk]
        o_ref[:, sl] = (out * g_ref[:, sl]).astype(o_ref.dtype)

    @pl.when(c == pl.num_programs(1) - 1)
    def _():
        s_out_ref[0] = s_ref[...]


def _rwkv(r, k, v, lw, a, g, row0, n_seq, seq_len, chunk, state0, k_k, k_a, r_k, lnx_g, lnx_b, group=4):
    d = D_MODEL
    n_groups = RWKV_HEADS // group
    width = group * RWKV_HEAD_DIM
    n_chunks = seq_len // chunk
    blk0 = row0 // chunk
    tok = pl.BlockSpec((chunk, d), lambda s, c: (blk0 + s * n_chunks + c, 0))
    vec = pl.BlockSpec((1, d), lambda s, c: (0, 0))
    s5 = state0.reshape(n_seq, n_groups, group, RWKV_HEAD_DIM, RWKV_HEAD_DIM)
    eye = jnp.eye(group, dtype=F32)
    s_bd = (s5[:, :, :, :, None, :] * eye[None, None, :, None, :, None]).reshape(n_seq, n_groups, width, width)
    state_spec = pl.BlockSpec((1, n_groups, width, width), lambda s, c: (s, 0, 0, 0))
    o, s_new = pl.pallas_call(
        functools.partial(_rwkv_kernel, chunk=chunk, group=group),
        grid=(n_seq, n_chunks),
        in_specs=[tok] * 6 + [vec] * 5 + [state_spec],
        out_specs=[pl.BlockSpec((chunk, d), lambda s, c: (s * n_chunks + c, 0)), state_spec],
        out_shape=[jax.ShapeDtypeStruct((n_seq * seq_len, d), BF16),
                   jax.ShapeDtypeStruct(s_bd.shape, F32)],
        scratch_shapes=[pltpu.VMEM((n_groups, width, width), F32)],
        compiler_params=_params("parallel", "arbitrary"),
        name="rwkv_wkv",
    )(r, k, v, lw, a, g, k_k.reshape(1, d), k_a.reshape(1, d), r_k.reshape(1, d),
      lnx_g.reshape(1, d), lnx_b.reshape(1, d), s_bd)
    s6 = s_new.reshape(n_seq, n_groups, group, RWKV_HEAD_DIM, group, RWKV_HEAD_DIM)
    diag = jnp.stack([s6[:, :, h, :, h, :] for h in range(group)], axis=2)
    return o, diag.reshape(state0.shape)


def _forward(x_prompt, x_sample, cache_k, cache_v, page_table, state_gdn_conv, state_gdn,
             state_rwkv_shift, state_rwkv, p, *, tm, tm_proj, sb_tq, sb_tk, pages_per_step,
             chunk_prompt, gdn_chunk_sample, rwkv_chunk_sample, ffn_tf):
    bp, t_p, d = x_prompt.shape
    bs, t_s, _ = x_sample.shape
    assert bp == 1
    n_p, n_s = bp * t_p, bs * t_s
    x = jnp.concatenate([x_prompt.reshape(n_p, d), x_sample.reshape(n_s, d)], axis=0)

    w_in = p['w_in0']
    cut = 3 * SB_WIDTH + 3 * GDN_WIDTH
    w_pad = jnp.concatenate(
        [w_in[:, :cut], w_in[:, cut + 2 * GDN_HEADS:], w_in[:, cut:cut + 2 * GDN_HEADS],
         jnp.zeros((d, COL_BLOCK - 2 * GDN_HEADS), F32)], axis=1).astype(BF16)
    y = _matmul(x, w_pad, tm, COL_BLOCK)
    q = y[:, :SB_WIDTH] * (SB_HEAD_DIM ** -0.5)
    k_rows = y[:, SB_WIDTH:2 * SB_WIDTH]
    v_rows = y[:, 2 * SB_WIDTH:3 * SB_WIDTH]
    gdn_rows = y[:, 3 * SB_WIDTH:3 * SB_WIDTH + 3 * GDN_WIDTH]

    o_sb_p = _sb_prompt(q[:n_p], k_rows[:n_p], v_rows[:n_p], p['sb_bias'], sb_tq, sb_tk)
    shape_s = (bs, t_s, SB_WIDTH)
    o_sb_s = _sb_sample(q[n_p:].reshape(shape_s), k_rows[n_p:].reshape(shape_s), v_rows[n_p:].reshape(shape_s),
                        cache_k, cache_v, page_table, p['sb_bias'], pages_per_step)
    o_sb = jnp.concatenate([o_sb_p, o_sb_s.reshape(n_s, SB_WIDTH)], axis=0)

    gdn_args = (p['gdn_conv_w'], p['gdn_a_log'], p['gdn_dt_bias'], p['gdn_norm_w'])
    o_gdn_p, gdn_state_p = _gdn(y, 0, bp, t_p, chunk_prompt,
                                jnp.zeros((bp, GDN_CONV - 1, 3 * GDN_WIDTH), F32),
                                jnp.zeros((bp,) + state_gdn.shape[1:], F32), *gdn_args)
    o_gdn_s, gdn_state_s = _gdn(y, n_p, bs, t_s, gdn_chunk_sample, state_gdn_conv, state_gdn, *gdn_args)
    o_gdn = jnp.concatenate([o_gdn_p, o_gdn_s], axis=0)
    conv_p = gdn_rows[:n_p].reshape(bp, t_p, -1)[:, t_p - (GDN_CONV - 1):]
    conv_s = gdn_rows[n_p:].reshape(bs, t_s, -1)[:, t_s - (GDN_CONV - 1):]

    w_out = p['w_out0'].astype(BF16)
    x = _proj_ln([o_sb, o_gdn], [w_out[:SB_WIDTH], w_out[SB_WIDTH:]], x,
                 p['ln_gamma'][0, 0], p['ln_beta'][0, 0], tm, "out_proj_ln")
    x = _ffn_ln(x, p['ffn_gate'].astype(BF16)[None], p['ffn_up'].astype(BF16)[None],
                p['ffn_down'].astype(BF16)[None], p['ln_gamma'][0, 1], p['ln_beta'][0, 1], None,
                tm, ffn_tf, "ffn_ln")

    x_p = x[:n_p].reshape(bp, t_p, d)
    x_s = x[n_p:].reshape(bs, t_s, d)
    prev_p = jnp.concatenate([jnp.zeros((bp, 1, d), F32), x_p[:, :-1]], axis=1)
    prev_s = jnp.concatenate([state_rwkv_shift[:, None, :], x_s[:, :-1]], axis=1)
    x_prev = jnp.concatenate([prev_p.reshape(n_p, d), prev_s.reshape(n_s, d)], axis=0)
    bf = lambda name: p[name].astype(BF16)
    r, k, v, lw, a, g = _rwkv_proj(
        x, x_prev, p['rwkv_mix'], bf('rwkv_w_r'), bf('rwkv_w_k'), bf('rwkv_w_v'), bf('rwkv_w1'), bf('rwkv_w2'),
        bf('rwkv_a1'), bf('rwkv_a2'), bf('rwkv_g1'), bf('rwkv_g2'),
        p['rwkv_w0'].reshape(1, d), p['rwkv_a0'].reshape(1, d), tm_proj)
    rwkv_args = (p['rwkv_k_k'], p['rwkv_k_a'], p['rwkv_r_k'], p['rwkv_lnx_g'], p['rwkv_lnx_b'])
    o_p, rwkv_state_p = _rwkv(r, k, v, lw, a, g, 0, bp, t_p, chunk_prompt,
                              jnp.zeros((bp,) + state_rwkv.shape[1:], F32), *rwkv_args)
    o_s, rwkv_state_s = _rwkv(r, k, v, lw, a, g, n_p, bs, t_s, rwkv_chunk_sample, state_rwkv, *rwkv_args)
    x1 = _proj_ln([jnp.concatenate([o_p, o_s], axis=0)], [bf('rwkv_w_o')], x,
                  p['ln_gamma'][1, 0], p['ln_beta'][1, 0], tm, "rwkv_out_ln")
    w_router = jnp.pad(p['moe_router'], ((0, 0), (0, 128 - N_EXPERTS)))
    out = _ffn_ln(x1, bf('moe_gate'), bf('moe_up'), bf('moe_down'), p['ln_gamma'][1, 1], p['ln_beta'][1, 1],
                  w_router, tm, ffn_tf, "moe_ln")

    heads = lambda rows, b_, t_: rows.reshape(b_, t_, SB_HEADS, SB_HEAD_DIM)
    return (out[:n_p].reshape(bp, t_p, d), out[n_p:].reshape(bs, t_s, d),
            heads(k_rows[:n_p], bp, t_p), heads(v_rows[:n_p], bp, t_p), conv_p, gdn_state_p,
            x_p[:, -1], rwkv_state_p,
            heads(k_rows[n_p:], bs, t_s), heads(v_rows[n_p:], bs, t_s), conv_s, gdn_state_s,
            x_s[:, -1], rwkv_state_s)


def kernel(x_prompt, x_sample, cache_k, cache_v, page_table, state_gdn_conv, state_gdn, state_rwkv_shift, state_rwkv, w_in0, sb_bias, gdn_conv_w, gdn_a_log, gdn_dt_bias, gdn_norm_w, w_out0, ffn_gate, ffn_up, ffn_down, rwkv_mix, rwkv_w_r, rwkv_w_k, rwkv_w_v, rwkv_w0, rwkv_w1, rwkv_w2, rwkv_a0, rwkv_a1, rwkv_a2, rwkv_g1, rwkv_g2, rwkv_k_k, rwkv_k_a, rwkv_r_k, rwkv_lnx_g, rwkv_lnx_b, rwkv_w_o, moe_router, moe_gate, moe_up, moe_down, ln_gamma, ln_beta):
    p = dict(w_in0=w_in0, sb_bias=sb_bias, gdn_conv_w=gdn_conv_w, gdn_a_log=gdn_a_log, gdn_dt_bias=gdn_dt_bias,
             gdn_norm_w=gdn_norm_w, w_out0=w_out0, ffn_gate=ffn_gate, ffn_up=ffn_up, ffn_down=ffn_down,
             rwkv_mix=rwkv_mix, rwkv_w_r=rwkv_w_r, rwkv_w_k=rwkv_w_k, rwkv_w_v=rwkv_w_v,
             rwkv_w0=rwkv_w0, rwkv_w1=rwkv_w1, rwkv_w2=rwkv_w2, rwkv_a0=rwkv_a0, rwkv_a1=rwkv_a1,
             rwkv_a2=rwkv_a2, rwkv_g1=rwkv_g1, rwkv_g2=rwkv_g2, rwkv_k_k=rwkv_k_k, rwkv_k_a=rwkv_k_a,
             rwkv_r_k=rwkv_r_k, rwkv_lnx_g=rwkv_lnx_g, rwkv_lnx_b=rwkv_lnx_b, rwkv_w_o=rwkv_w_o,
             moe_router=moe_router, moe_gate=moe_gate, moe_up=moe_up, moe_down=moe_down,
             ln_gamma=ln_gamma, ln_beta=ln_beta)
    return _forward(x_prompt, x_sample, cache_k, cache_v, page_table, state_gdn_conv, state_gdn,
                    state_rwkv_shift, state_rwkv, p, tm=640, tm_proj=320, sb_tq=512, sb_tk=256,
                    pages_per_step=4, chunk_prompt=64, gdn_chunk_sample=8, rwkv_chunk_sample=8, ffn_tf=1408)
```

```python
import functools
import math

import jax
import jax.numpy as jnp
import numpy as np
from jax import lax
from jax.experimental import pallas as pl
from jax.experimental.pallas import tpu as pltpu

F32 = jnp.float32
BF16 = jnp.bfloat16

D_MODEL = 1024
SB_HEADS = 8
SB_HEAD_DIM = 64
SB_WIDTH = SB_HEADS * SB_HEAD_DIM
GDN_HEADS = 4
GDN_HEAD_DIM = 128
GDN_WIDTH = GDN_HEADS * GDN_HEAD_DIM
GDN_CONV = 4
RWKV_HEAD_DIM = 64
RWKV_HEADS = D_MODEL // RWKV_HEAD_DIM
RWKV_GN_EPS = 64e-5
N_EXPERTS = 8
DEPTH = 2
DEEPNORM_ALPHA = (2 * DEPTH) ** 0.25
LN_EPS = 1e-5
NORM_EPS = 1e-6
LOG2E = math.log2(math.e)

IN0_COLS = 4096
COL_BLOCK = 512
VMEM_LIMIT_BYTES = 56 * 1024 * 1024

_NT = (((1,), (1,)), ((), ()))
_TN = (((0,), (0,)), ((), ()))


def _params(*sem):
    return pltpu.CompilerParams(dimension_semantics=sem, vmem_limit_bytes=VMEM_LIMIT_BYTES)


def _dot(a, b):
    return jnp.dot(a.astype(BF16), b.astype(BF16), preferred_element_type=F32)


def _dot_nt(a, b):
    return lax.dot_general(a.astype(BF16), b.astype(BF16), _NT, preferred_element_type=F32)


def _dot_tn(a, b):
    return lax.dot_general(a.astype(BF16), b.astype(BF16), _TN, preferred_element_type=F32)


def _dot_exact_lhs(a01, x):
    a = a01.astype(BF16)
    x1 = x.astype(BF16)
    r1 = x - x1.astype(F32)
    x2 = r1.astype(BF16)
    x3 = (r1 - x2.astype(F32)).astype(BF16)
    out = jnp.dot(a, x1, preferred_element_type=F32)
    out = out + jnp.dot(a, x2, preferred_element_type=F32)
    return out + jnp.dot(a, x3, preferred_element_type=F32)


def _softplus(z):
    return jnp.maximum(z, 0.0) + jnp.log1p(jnp.exp(-jnp.abs(z)))


def _sigmoid(z):
    return 1.0 / (1.0 + jnp.exp(-z))


def _silu(z):
    return z * _sigmoid(z)


def _layer_norm(x, g, b):
    mu = jnp.mean(x, axis=-1, keepdims=True)
    xc = x - mu
    var = jnp.mean(xc * xc, axis=-1, keepdims=True)
    return xc * lax.rsqrt(var + LN_EPS) * g + b


def _inv_i_minus(n, size, block):
    return _inv_i_minus_many([n], size, block)[0]


def _inv_i_minus_many(ns, size, block):
    rows = lax.broadcasted_iota(jnp.int32, (size, size), 0)
    cols = lax.broadcasted_iota(jnp.int32, (size, size), 1)
    eye = jnp.where(rows == cols, 1.0, 0.0)
    ps = [eye + n for n in ns]
    ys = list(ns)
    for _ in range(max(0, int(math.ceil(math.log2(block))) - 1)):
        ys = [_dot(y, y) for y in ys]
        ps = [p + _dot(p, y) for p, y in zip(ps, ys)]
    return ps


def _matmul_kernel(x_ref, w_ref, o_ref):
    o_ref[...] = jnp.dot(x_ref[...].astype(BF16), w_ref[...], preferred_element_type=F32)


def _matmul(x, w, tm, tn):
    n, k = x.shape
    m = w.shape[1]
    return pl.pallas_call(
        _matmul_kernel,
        grid=(n // tm, m // tn),
        in_specs=[pl.BlockSpec((tm, k), lambda i, j: (i, 0)),
                  pl.BlockSpec((k, tn), lambda i, j: (0, j))],
        out_specs=pl.BlockSpec((tm, tn), lambda i, j: (i, j)),
        out_shape=jax.ShapeDtypeStruct((n, m), F32),
        compiler_params=_params("parallel", "parallel"),
        name="in_proj",
    )(x, w)


def _proj_ln_kernel(*refs, n_in):
    a_refs = refs[:n_in]
    w_refs = refs[n_in:2 * n_in]
    x_ref, g_ref, b_ref, o_ref = refs[2 * n_in:]
    h = jnp.dot(a_refs[0][...], w_refs[0][...], preferred_element_type=F32)
    for a_ref, w_ref in zip(a_refs[1:], w_refs[1:]):
        h = h + jnp.dot(a_ref[...], w_ref[...], preferred_element_type=F32)
    o_ref[...] = _layer_norm(DEEPNORM_ALPHA * x_ref[...] + h, g_ref[...], b_ref[...])


def _proj_ln(acts, weights, x, gamma, beta, tm, name):
    n, d = x.shape
    n_in = len(acts)
    in_specs = [pl.BlockSpec((tm, a.shape[1]), lambda i: (i, 0)) for a in acts]
    in_specs += [pl.BlockSpec(w.shape, lambda i: (0, 0)) for w in weights]
    in_specs += [pl.BlockSpec((tm, d), lambda i: (i, 0)),
                 pl.BlockSpec((1, d), lambda i: (0, 0)),
                 pl.BlockSpec((1, d), lambda i: (0, 0))]
    return pl.pallas_call(
        functools.partial(_proj_ln_kernel, n_in=n_in),
        grid=(n // tm,),
        in_specs=in_specs,
        out_specs=pl.BlockSpec((tm, d), lambda i: (i, 0)),
        out_shape=jax.ShapeDtypeStruct((n, d), F32),
        compiler_params=_params("parallel"),
        name=name,
    )(*acts, *weights, x, gamma.reshape(1, d), beta.reshape(1, d))


def _top2_gates(logits):
    lane = lax.broadcasted_iota(jnp.int32, logits.shape, 1).astype(F32)
    big = float(logits.shape[1])
    lg = jnp.where(lane < N_EXPERTS, logits, -jnp.inf)
    m1 = jnp.max(lg, axis=-1, keepdims=True)
    i1 = jnp.min(jnp.where(lg == m1, lane, big), axis=-1, keepdims=True)
    lg2 = jnp.where(lane == i1, -jnp.inf, lg)
    m2 = jnp.max(lg2, axis=-1, keepdims=True)
    i2 = jnp.min(jnp.where(lg2 == m2, lane, big), axis=-1, keepdims=True)
    e2 = jnp.exp(m2 - m1)
    den = 1.0 + e2
    return jnp.where(lane == i1, 1.0 / den, 0.0) + jnp.where(lane == i2, e2 / den, 0.0)


def _ffn_ln_kernel(x_ref, wg_ref, wu_ref, wd_ref, g_ref, b_ref, *rest, routed):
    if routed:
        wr_ref, o_ref, acc_ref, gate_ref = rest
    else:
        o_ref, acc_ref = rest
    e = pl.program_id(1)
    f = pl.program_id(2)
    first = jnp.logical_and(e == 0, f == 0)
    last = jnp.logical_and(e == pl.num_programs(1) - 1, f == pl.num_programs(2) - 1)

    @pl.when(first)
    def _():
        acc_ref[...] = jnp.zeros_like(acc_ref)
        if routed:
            logits = jnp.dot(x_ref[...], wr_ref[...], preferred_element_type=F32,
                             precision=lax.Precision.HIGHEST)
            gate_ref[...] = _top2_gates(logits)

    xb = x_ref[...].astype(BF16)
    hg = jnp.dot(xb, wg_ref[0], preferred_element_type=F32)
    hu = jnp.dot(xb, wu_ref[0], preferred_element_type=F32)
    y = jnp.dot((_silu(hg) * hu).astype(BF16), wd_ref[0], preferred_element_type=F32)
    if routed:
        gates = gate_ref[...]
        lane = lax.broadcasted_iota(jnp.int32, gates.shape, 1)
        y = y * jnp.sum(jnp.where(lane == e, gates, 0.0), axis=-1, keepdims=True)
    acc_ref[...] += y

    @pl.when(last)
    def _():
        o_ref[...] = _layer_norm(DEEPNORM_ALPHA * x_ref[...] + acc_ref[...], g_ref[...], b_ref[...])


def _ffn_ln(x, w_gate, w_up, w_down, gamma, beta, w_router, tm, tf, name):
    n, d = x.shape
    n_e, _, d_ff = w_gate.shape
    routed = w_router is not None
    in_specs = [pl.BlockSpec((tm, d), lambda i, e, f: (i, 0)),
                pl.BlockSpec((1, d, tf), lambda i, e, f: (e, 0, f)),
                pl.BlockSpec((1, d, tf), lambda i, e, f: (e, 0, f)),
                pl.BlockSpec((1, tf, d), lambda i, e, f: (e, f, 0)),
                pl.BlockSpec((1, d), lambda i, e, f: (0, 0)),
                pl.BlockSpec((1, d), lambda i, e, f: (0, 0))]
    args = [x, w_gate, w_up, w_down, gamma.reshape(1, d), beta.reshape(1, d)]
    scratch = [pltpu.VMEM((tm, d), F32)]
    if routed:
        in_specs.append(pl.BlockSpec(w_router.shape, lambda i, e, f: (0, 0)))
        args.append(w_router)
        scratch.append(pltpu.VMEM((tm, w_router.shape[1]), F32))
    return pl.pallas_call(
        functools.partial(_ffn_ln_kernel, routed=routed),
        grid=(n // tm, n_e, d_ff // tf),
        in_specs=in_specs,
        out_specs=pl.BlockSpec((tm, d), lambda i, e, f: (i, 0)),
        out_shape=jax.ShapeDtypeStruct((n, d), F32),
        scratch_shapes=scratch,
        compiler_params=_params("parallel", "arbitrary", "arbitrary"),
        name=name,
    )(*args)


def _rwkv_proj_kernel(x_ref, xp_ref, mix_ref, wr_ref, wk_ref, wv_ref, w1_ref, w2_ref, a1_ref, a2_ref,
                      g1_ref, g2_ref, w0_ref, a0_ref, r_ref, k_ref, v_ref, lw_ref, a_ref, g_ref):
    x = x_ref[...]
    xx = xp_ref[...] - x
    mixed = lambda i: (x + xx * mix_ref[i:i + 1, :]).astype(BF16)
    r_ref[...] = jnp.dot(mixed(0), wr_ref[...], preferred_element_type=F32)
    w_lora = _dot(jnp.tanh(jnp.dot(mixed(1), w1_ref[...], preferred_element_type=F32)), w2_ref[...])
    lw_ref[...] = -jnp.exp(-_softplus(-(w0_ref[...] + w_lora)) - 0.5)
    k_ref[...] = jnp.dot(mixed(2), wk_ref[...], preferred_element_type=F32)
    v_ref[...] = jnp.dot(mixed(3), wv_ref[...], preferred_element_type=F32)
    a_lora = _dot(jnp.dot(mixed(4), a1_ref[...], preferred_element_type=F32), a2_ref[...])
    a_ref[...] = _sigmoid(a0_ref[...] + a_lora)
    g_ref[...] = _dot(_sigmoid(jnp.dot(mixed(5), g1_ref[...], preferred_element_type=F32)), g2_ref[...])


def _rwkv_proj(x, x_prev, mix, w_r, w_k, w_v, w1, w2, a1, a2, g1, g2, w0, a0, tm):
    n, d = x.shape
    tok = pl.BlockSpec((tm, d), lambda i: (i, 0))
    full = lambda a: pl.BlockSpec(a.shape, lambda i: (0, 0))
    consts = [mix, w_r, w_k, w_v, w1, w2, a1, a2, g1, g2, w0, a0]
    return pl.pallas_call(
        _rwkv_proj_kernel,
        grid=(n // tm,),
        in_specs=[tok, tok] + [full(c) for c in consts],
        out_specs=[tok] * 6,
        out_shape=[jax.ShapeDtypeStruct((n, d), F32)] * 6,
        compiler_params=_params("parallel"),
        name="rwkv_proj",
    )(x, x_prev, *consts)


def _suffix_sum_rows8(x):
    row = lax.broadcasted_iota(jnp.int32, x.shape, 0)
    for sh in (1, 2, 4):
        shifted = pltpu.roll(x, 8 - sh, axis=0)
        x = x + jnp.where(row < 8 - sh, shifted, 0.0)
    return x


def _sb_block(z_ref, r_ref, a_ref, carry, visible):
    tk, tq = z_ref.shape
    nsub = tk // 8
    run = jnp.zeros((8, tq), F32)
    for s in reversed(range(nsub)):
        rows = slice(s * 8, (s + 1) * 8)
        z = z_ref[rows, :]
        neg_abs = pltpu.bitcast(pltpu.bitcast(z, jnp.uint32) | jnp.uint32(0x80000000), F32)
        sp = jnp.maximum(z, 0.0) + jnp.log2(1.0 + jnp.exp2(neg_abs))
        vis = visible(s * 8, 8)
        if vis is None:
            run = run + sp
            r_ref[rows, :] = z - run
        else:
            r_ref[rows, :] = (z - sp) - run
            run = run + jnp.where(vis, sp, 0.0)
    incl = _suffix_sum_rows8(run)
    offset = incl - run + carry
    offset2 = jnp.concatenate([offset, offset], axis=0)
    for s in range(nsub // 2):
        rows = slice(s * 16, (s + 1) * 16)
        a = jnp.exp2(r_ref[rows, :] - offset2)
        vis = visible(s * 16, 16)
        if vis is not None:
            a = jnp.where(vis, a, 0.0)
        a_ref[rows, :] = a.astype(BF16)
    return carry + incl[0:1]


def _sb_prompt_kernel(q_ref, k_ref, vt_ref, o_ref, z0_ref, z1_ref, a0_ref, a1_ref, *, tq, tk):
    assert tq == 2 * tk
    i = pl.program_id(1)
    q = q_ref[0]
    nsub = tk // 8

    def scores(j):
        start = pl.multiple_of(jnp.maximum(j, 0) * tk, tk)
        return lax.dot_general(k_ref[0, pl.ds(start, tk), :], q, _NT, preferred_element_type=F32)

    def weighted_v(j, a_ref):
        start = pl.multiple_of(j * tk, tk)
        return jnp.dot(vt_ref[0, :, pl.ds(start, tk)], a_ref[...], preferred_element_type=F32)

    def weights(z_ref, a_ref, j, carry, masked):
        def visible(first_row, n_rows):
            if not masked:
                return None
            row = first_row + lax.broadcasted_iota(jnp.int32, (n_rows, tq), 0)
            col = lax.broadcasted_iota(jnp.int32, (n_rows, tq), 1)
            k_pos = j * tk + (row % 8) * nsub + row // 8
            return k_pos < i * tq + col
        return _sb_block(z_ref, z_ref, a_ref, carry, visible)

    newest = 2 * i + 1
    acc = jnp.zeros((SB_HEAD_DIM, tq), F32)
    carry = jnp.zeros((1, tq), F32)
    z0_ref[...] = scores(newest)
    z1_ref[...] = scores(newest - 1)
    carry = weights(z0_ref, a0_ref, newest, carry, True)
    z0_ref[...] = scores(newest - 2)
    acc = acc + weighted_v(newest, a0_ref)
    carry = weights(z1_ref, a1_ref, newest - 1, carry, True)

    def body(t, state):
        acc, carry = state
        blk = newest - 2 - 2 * t
        z1_ref[...] = scores(blk - 1)
        acc = acc + weighted_v(blk + 1, a1_ref)
        carry = weights(z0_ref, a0_ref, blk, carry, False)
        z0_ref[...] = scores(blk - 2)
        acc = acc + weighted_v(blk, a0_ref)
        carry = weights(z1_ref, a1_ref, blk - 1, carry, False)
        return acc, carry

    acc, carry = lax.fori_loop(0, i, body, (acc, carry))
    acc = acc + weighted_v(0, a1_ref)
    o_ref[0] = acc.astype(o_ref.dtype)


def _bf16_pieces(x, n):
    out = []
    for _ in range(n):
        piece = x.astype(BF16)
        out.append(piece)
        x = x - piece.astype(F32)
    return out


def _sb_prompt(q, k, v, bias, tq, tk):
    t = q.shape[0]
    nblk, nsub = t // tk, tk // 8
    pad = 128 - SB_HEAD_DIM
    qh = (q * (SB_HEAD_DIM ** -0.5 * LOG2E)).astype(BF16).reshape(t, SB_HEADS, SB_HEAD_DIM).transpose(1, 0, 2)
    bias_cols = jnp.stack(_bf16_pieces(bias.astype(F32) * LOG2E, 3), axis=-1)
    q_extra = jnp.pad(bias_cols, ((0, 0), (0, pad - 3)))[:, None, :]
    q_aug = jnp.concatenate([qh, jnp.broadcast_to(q_extra, (SB_HEADS, t, pad))], axis=-1)
    k5 = k.astype(BF16).reshape(nblk, 8, nsub, SB_HEADS, SB_HEAD_DIM)
    kh = k5.transpose(3, 0, 2, 1, 4).reshape(SB_HEADS, t, SB_HEAD_DIM)
    k_extra = jnp.pad(jnp.ones((3,), BF16), (0, pad - 3))
    k_aug = jnp.concatenate([kh, jnp.broadcast_to(k_extra, (SB_HEADS, t, pad))], axis=-1)
    v5 = v.astype(BF16).reshape(nblk, 8, nsub, SB_HEADS, SB_HEAD_DIM)
    vt = v5.transpose(3, 4, 0, 2, 1).reshape(SB_HEADS, SB_HEAD_DIM, t)
    out_t = pl.pallas_call(
        functools.partial(_sb_prompt_kernel, tq=tq, tk=tk),
        grid=(SB_HEADS, t // tq),
        in_specs=[pl.BlockSpec((1, tq, 128), lambda h, i: (h, i, 0)),
                  pl.BlockSpec((1, t, 128), lambda h, i: (h, 0, 0)),
                  pl.BlockSpec((1, SB_HEAD_DIM, t), lambda h, i: (h, 0, 0))],
        out_specs=pl.BlockSpec((1, SB_HEAD_DIM, tq), lambda h, i: (h, 0, i)),
        out_shape=jax.ShapeDtypeStruct((SB_HEADS, SB_HEAD_DIM, t), BF16),
        scratch_shapes=[pltpu.VMEM((tk, tq), F32), pltpu.VMEM((tk, tq), F32),
                        pltpu.VMEM((tk, tq), BF16), pltpu.VMEM((tk, tq), BF16)],
        compiler_params=_params("parallel", "parallel"),
        name="sb_prompt",
    )(q_aug, k_aug, vt)
    return out_t.transpose(2, 0, 1).reshape(t, SB_WIDTH)


def _sb_sample_kernel(pt_ref, qbd_ref, bias_ref, kn_ref, vn_ref, *rest, pages_per_step, n_q):
    del pt_ref
    k_refs = rest[:pages_per_step]
    v_refs = rest[pages_per_step:2 * pages_per_step]
    o_ref, acc_ref, carry_ref = rest[2 * pages_per_step:]
    g = pl.program_id(1)
    qbd = qbd_ref[0]
    bias = bias_ref[...]
    n_col = qbd.shape[0]

    @pl.when(g == 0)
    def _():
        z = _dot_nt(kn_ref[0], qbd) + bias
        row = lax.broadcasted_iota(jnp.int32, z.shape, 0)
        col = lax.broadcasted_iota(jnp.int32, z.shape, 1)
        vis = row < col % n_q
        sp = jnp.where(vis, _softplus(z), 0.0)
        incl = _suffix_sum_rows8(sp)
        a = jnp.where(vis, jnp.exp(z - _softplus(z) - (incl - sp)), 0.0)
        acc_ref[...] = _dot_tn(a, vn_ref[0])
        carry_ref[...] = incl[0:1]

    acc = acc_ref[...]
    carry = carry_ref[...]
    page = k_refs[0].shape[1] // SB_HEADS
    row = lax.broadcasted_iota(jnp.int32, (page, page), 0)
    col = lax.broadcasted_iota(jnp.int32, (page, page), 1)
    later = jnp.where(col > row, 1.0, 0.0)
    rows = lambda ref: jnp.concatenate(
        [ref[0, pl.ds(h, page, stride=SB_HEADS), :] for h in range(SB_HEADS)], axis=-1)
    zs = [_dot_nt(rows(k_ref), qbd) + bias for k_ref in k_refs]
    sps = [_softplus(z) for z in zs]
    afters = [_dot_exact_lhs(later, sp) for sp in sps]
    for z, sp, after, v_ref in zip(zs, sps, afters, v_refs):
        a = jnp.exp(z - sp - after - carry)
        acc = acc + _dot_tn(a, rows(v_ref))
        carry = carry + jnp.sum(sp, axis=0, keepdims=True)
    acc_ref[...] = acc
    carry_ref[...] = carry

    @pl.when(g == pl.num_programs(1) - 1)
    def _():
        r = lax.broadcasted_iota(jnp.int32, acc.shape, 0)
        c = lax.broadcasted_iota(jnp.int32, acc.shape, 1)
        own = jnp.where(r // n_q == c // SB_HEAD_DIM, acc, 0.0)
        o_ref[0] = jnp.sum(own.reshape(SB_HEADS, n_q, SB_WIDTH), axis=0).astype(o_ref.dtype)


def _sb_sample(q, k_new, v_new, cache_k, cache_v, page_table, bias, pages_per_step):
    b, n_q, _ = q.shape
    n_pages = page_table.shape[1]
    n_pool, page = cache_k.shape[:2]
    ck = cache_k.reshape(n_pool, page * SB_HEADS, SB_HEAD_DIM)
    cv = cache_v.reshape(n_pool, page * SB_HEADS, SB_HEAD_DIM)
    q4 = (q * SB_HEAD_DIM ** -0.5).astype(BF16).reshape(b, n_q, SB_HEADS, SB_HEAD_DIM).transpose(0, 2, 1, 3)
    eye = jnp.eye(SB_HEADS, dtype=BF16)
    qbd = (q4[:, :, :, None, :] * eye[None, :, None, :, None]).reshape(b, SB_HEADS * n_q, SB_WIDTH)
    bias_row = jnp.repeat(bias.astype(F32), n_q).reshape(1, SB_HEADS * n_q)
    steps = n_pages // pages_per_step

    def page_map(u):
        return lambda s, g, pt: (pt[s, n_pages - 1 - (g * pages_per_step + u)], 0, 0)

    page_specs = [pl.BlockSpec((1, page * SB_HEADS, SB_HEAD_DIM), page_map(u)) for u in range(pages_per_step)]
    per_seq = lambda shape: pl.BlockSpec((1,) + shape, lambda s, g, pt: (s, 0, 0))
    grid_spec = pltpu.PrefetchScalarGridSpec(
        num_scalar_prefetch=1,
        grid=(b, steps),
        in_specs=[per_seq((SB_HEADS * n_q, SB_WIDTH)),
                  pl.BlockSpec((1, SB_HEADS * n_q), lambda s, g, pt: (0, 0)),
                  per_seq((n_q, SB_WIDTH)), per_seq((n_q, SB_WIDTH))] + page_specs + page_specs,
        out_specs=per_seq((n_q, SB_WIDTH)),
        scratch_shapes=[pltpu.VMEM((SB_HEADS * n_q, SB_WIDTH), F32),
                        pltpu.VMEM((1, SB_HEADS * n_q), F32)],
    )
    return pl.pallas_call(
        functools.partial(_sb_sample_kernel, pages_per_step=pages_per_step, n_q=n_q),
        grid_spec=grid_spec,
        out_shape=jax.ShapeDtypeStruct((b, n_q, SB_WIDTH), BF16),
        compiler_params=_params("parallel", "arbitrary"),
        name="sb_sample",
    )(page_table, qbd, bias_row, k_new, v_new, *([ck] * pages_per_step), *([cv] * pages_per_step))


def _gdn_kernel(hp_ref, qkv_ref, gate_ref, ab_ref, cbuf_ref, cw_ref, nw_ref, s0_ref,
                o_ref, s_out_ref, s_ref, carry_ref, *, chunk):
    c = pl.program_id(1)

    @pl.when(c == 0)
    def _():
        s_ref[...] = s0_ref[0]
        carry_ref[...] = cbuf_ref[0]

    x = qkv_ref[...]
    ext = jnp.concatenate([carry_ref[...], x], axis=0)
    conv = ext[5:5 + chunk] * cw_ref[0:1, :]
    for i in range(1, GDN_CONV):
        conv = conv + ext[5 + i:5 + i + chunk] * cw_ref[i:i + 1, :]
    carry_ref[...] = ext[chunk:chunk + 8]
    act = _silu(conv)

    rows = lax.broadcasted_iota(jnp.int32, (chunk, chunk), 0)
    cols = lax.broadcasted_iota(jnp.int32, (chunk, chunk), 1)
    lower = cols <= rows
    strict = cols < rows
    lower01 = jnp.where(lower, 1.0, 0.0)
    ab = ab_ref[...]
    d = GDN_HEAD_DIM
    heads = range(GDN_HEADS)
    head_cols = lambda x, base, h: x[:, base + h * d:base + (h + 1) * d]
    l2n = lambda x: x * lax.rsqrt(jnp.sum(x * x, axis=-1, keepdims=True) + NORM_EPS)
    qn = [l2n(head_cols(act, 0, h)) * (d ** -0.5) for h in heads]
    kn = [l2n(head_cols(act, GDN_WIDTH, h)) for h in heads]
    vh = [head_cols(act, 2 * GDN_WIDTH, h) for h in heads]
    beta = [_sigmoid(ab[:, GDN_HEADS + h:GDN_HEADS + h + 1]) for h in heads]
    g = [-jnp.exp(hp_ref[0:1, h:h + 1]) * _softplus(ab[:, h:h + 1] + hp_ref[1:2, h:h + 1]) for h in heads]
    gc = [_dot_exact_lhs(lower01, jnp.broadcast_to(g[h], (chunk, d))) for h in heads]
    gi = [gc[h][:, :chunk] if chunk <= d else jnp.broadcast_to(gc[h][:, :1], (chunk, chunk)) for h in heads]
    decay = [jnp.exp(jnp.where(lower, gi[h] - gi[h].T, -jnp.inf)) for h in heads]
    kb = [kn[h] * beta[h] for h in heads]
    tri = [jnp.where(strict, _dot_nt(kb[h], kn[h]) * decay[h], 0.0) for h in heads]
    attn = [jnp.where(lower, _dot_nt(qn[h], kn[h]) * decay[h], 0.0) for h in heads]
    t_inv = _inv_i_minus_many([-t for t in tri], chunk, chunk)
    e_gc = [jnp.exp(gc[h]) for h in heads]
    uw = [_dot(t_inv[h], jnp.concatenate([vh[h] * beta[h], kb[h] * e_gc[h]], axis=-1)) for h in heads]
    s = [s_ref[h] for h in heads]
    v_new = [uw[h][:, :d] - _dot_nt(uw[h][:, d:], s[h]) for h in heads]
    o = [_dot_nt(qn[h] * e_gc[h], s[h]) + _dot(attn[h], v_new[h]) for h in heads]
    for h in heads:
        g_last = gc[h][chunk - 1:chunk, :]
        s_ref[h] = s[h] * jnp.exp(g_last) + _dot_tn(v_new[h], kn[h] * jnp.exp(g_last - gc[h]))
    for h in heads:
        on = o[h] * lax.rsqrt(jnp.mean(o[h] * o[h], axis=-1, keepdims=True) + NORM_EPS) * nw_ref[...]
        o_ref[:, h * d:(h + 1) * d] = (on * _silu(gate_ref[:, h * d:(h + 1) * d])).astype(o_ref.dtype)

    @pl.when(c == pl.num_programs(1) - 1)
    def _():
        s_out_ref[0] = s_ref[...]


def _gdn(y_all, row0, n_seq, seq_len, chunk, conv_buf, state0, conv_w, a_log, dt_bias, norm_w):
    n_chunks = seq_len // chunk
    blk0 = row0 // chunk
    row_map = lambda col: (lambda s, c: (blk0 + s * n_chunks + c, col))
    cbuf = jnp.pad(conv_buf, ((0, 0), (8 - (GDN_CONV - 1), 0), (0, 0)))
    cw = jnp.pad(conv_w, ((0, 8 - GDN_CONV), (0, 0)))
    head_params = jnp.zeros((8, 128), F32).at[0, :GDN_HEADS].set(a_log).at[1, :GDN_HEADS].set(dt_bias)
    state_spec = pl.BlockSpec((1, GDN_HEADS, GDN_HEAD_DIM, GDN_HEAD_DIM), lambda s, c: (s, 0, 0, 0))
    o, s_new = pl.pallas_call(
        functools.partial(_gdn_kernel, chunk=chunk),
        grid=(n_seq, n_chunks),
        in_specs=[pl.BlockSpec((8, 128), lambda s, c: (0, 0)),
                  pl.BlockSpec((chunk, 3 * GDN_WIDTH), row_map(1)),
                  pl.BlockSpec((chunk, COL_BLOCK), row_map(6)),
                  pl.BlockSpec((chunk, COL_BLOCK), row_map(7)),
                  pl.BlockSpec((1, 8, 3 * GDN_WIDTH), lambda s, c: (s, 0, 0)),
                  pl.BlockSpec((8, 3 * GDN_WIDTH), lambda s, c: (0, 0)),
                  pl.BlockSpec((1, GDN_HEAD_DIM), lambda s, c: (0, 0)),
                  state_spec],
        out_specs=[pl.BlockSpec((chunk, GDN_WIDTH), lambda s, c: (s * n_chunks + c, 0)), state_spec],
        out_shape=[jax.ShapeDtypeStruct((n_seq * seq_len, GDN_WIDTH), BF16),
                   jax.ShapeDtypeStruct(state0.shape, F32)],
        scratch_shapes=[pltpu.VMEM((GDN_HEADS, GDN_HEAD_DIM, GDN_HEAD_DIM), F32),
                        pltpu.VMEM((8, 3 * GDN_WIDTH), F32)],
        compiler_params=_params("parallel", "arbitrary"),
        name="gdn",
    )(head_params, y_all, y_all, y_all, cbuf, cw, norm_w.reshape(1, GDN_HEAD_DIM),
      jnp.swapaxes(state0, -1, -2))
    return o, jnp.swapaxes(s_new, -1, -2)


def _rwkv_kernel(r_ref, k_ref, v_ref, lw_ref, a_ref, g_ref, kk_ref, ka_ref, rk_ref, lg_ref, lb_ref,
                 s0_ref, o_ref, s_out_ref, s_ref, *, chunk, group):
    c = pl.program_id(1)

    @pl.when(c == 0)
    def _():
        s_ref[...] = s0_ref[0]

    hd = RWKV_HEAD_DIM
    width = group * hd
    size = group * chunk
    rows = lax.broadcasted_iota(jnp.int32, (size, size), 0)
    cols = lax.broadcasted_iota(jnp.int32, (size, size), 1)
    same = rows // chunk == cols // chunk
    lower = jnp.logical_and(same, cols <= rows)
    strict = jnp.logical_and(same, cols < rows)
    r2 = lax.broadcasted_iota(jnp.int32, (size, width), 0)
    c2 = lax.broadcasted_iota(jnp.int32, (size, width), 1)
    own = r2 // chunk == c2 // hd
    groups = range(D_MODEL // width)
    cols_of = lambda gi: slice(gi * width, (gi + 1) * width)
    stack = lambda x, gi: jnp.where(own, jnp.concatenate([x[:, cols_of(gi)]] * group, axis=0), 0.0)

    tr = lax.broadcasted_iota(jnp.int32, (chunk, chunk), 0)
    tc = lax.broadcasted_iota(jnp.int32, (chunk, chunk), 1)
    lw_all = lw_ref[...]
    cw_all = _dot_exact_lhs(jnp.where(tc <= tr, 1.0, 0.0), lw_all)
    tot_all = jnp.sum(lw_all, axis=0, keepdims=True)

    r = [stack(r_ref[...], gi) for gi in groups]
    v = [stack(v_ref[...], gi) for gi in groups]
    a = [stack(a_ref[...], gi) for gi in groups]
    lw = [stack(lw_all, gi) for gi in groups]
    cw = [stack(cw_all, gi) for gi in groups]
    k_in = [stack(k_ref[...], gi) for gi in groups]
    kk = [k_in[gi] * kk_ref[:, cols_of(gi)] for gi in groups]
    kk = [x * lax.rsqrt(jnp.sum(x * x, axis=-1, keepdims=True) + NORM_EPS) for x in kk]
    k = [k_in[gi] * (1.0 + (a[gi] - 1.0) * ka_ref[:, cols_of(gi)]) for gi in groups]
    b = [kk[gi] * a[gi] for gi in groups]
    tot = [tot_all[:, cols_of(gi)] for gi in groups]
    e_neg = [jnp.exp(-cw[gi]) for gi in groups]
    e_end = [jnp.exp(jnp.where(own, tot[gi] - cw[gi], 0.0)) for gi in groups]
    at = [-kk[gi] * jnp.exp(cw[gi] - lw[gi]) for gi in groups]
    rt = [r[gi] * jnp.exp(cw[gi]) for gi in groups]
    bt = [b[gi] * e_neg[gi] for gi in groups]
    kt = [k[gi] * e_neg[gi] for gi in groups]
    a_ab = [jnp.where(strict, _dot_nt(at[gi], bt[gi]), 0.0) for gi in groups]
    a_ak = [jnp.where(strict, _dot_nt(at[gi], kt[gi]), 0.0) for gi in groups]
    a_rb = [jnp.where(lower, _dot_nt(rt[gi], bt[gi]), 0.0) for gi in groups]
    a_rk = [jnp.where(lower, _dot_nt(rt[gi], kt[gi]), 0.0) for gi in groups]
    t_inv = _inv_i_minus_many(a_ab, size, chunk)
    s = [s_ref[gi] for gi in groups]
    rhs = [_dot_nt(at[gi], s[gi]) + _dot(a_ak[gi], v[gi]) for gi in groups]
    o_past = [_dot_nt(rt[gi], s[gi]) + _dot(a_rk[gi], v[gi]) for gi in groups]
    u = [_dot(t_inv[gi], rhs[gi]) for gi in groups]
    o = [o_past[gi] + _dot(a_rb[gi], u[gi]) for gi in groups]
    for gi in groups:
        s_ref[gi] = (s[gi] * jnp.exp(tot[gi]) + _dot_tn(u[gi], b[gi] * e_end[gi])
                     + _dot_tn(v[gi], k[gi] * e_end[gi]))
    for gi in groups:
        sl = cols_of(gi)
        mu = jnp.sum(o[gi], axis=-1, keepdims=True) * (1.0 / hd)
        oc = jnp.where(own, o[gi] - mu, 0.0)
        var = jnp.sum(oc * oc, axis=-1, keepdims=True) * (1.0 / hd)
        on = oc * lax.rsqrt(var + RWKV_GN_EPS) * lg_ref[:, sl] + jnp.where(own, lb_ref[:, sl], 0.0)
        on = on + jnp.sum(r[gi] * k[gi] * rk_ref[:, sl], axis=-1, keepdims=True) * v[gi]
        out = on[0:chunk]
        for hh in range(1, group):
            out = out + on[hh * chunk:(hh + 1) * chunk]
        o_ref[:, sl] = (out * g_ref[:, sl]).astype(o_ref.dtype)

    @pl.when(c == pl.num_programs(1) - 1)
    def _():
        s_out_ref[0] = s_ref[...]


def _rwkv(r, k, v, lw, a, g, row0, n_seq, seq_len, chunk, state0, k_k, k_a, r_k, lnx_g, lnx_b, group=4):
    d = D_MODEL
    n_groups = RWKV_HEADS // group
    width = group * RWKV_HEAD_DIM
    n_chunks = seq_len // chunk
    blk0 = row0 // chunk
    tok = pl.BlockSpec((chunk, d), lambda s, c: (blk0 + s * n_chunks + c, 0))
    vec = pl.BlockSpec((1, d), lambda s, c: (0, 0))
    s5 = state0.reshape(n_seq, n_groups, group, RWKV_HEAD_DIM, RWKV_HEAD_DIM)
    eye = jnp.eye(group, dtype=F32)
    s_bd = (s5[:, :, :, :, None, :] * eye[None, None, :, None, :, None]).reshape(n_seq, n_groups, width, width)
    state_spec = pl.BlockSpec((1, n_groups, width, width), lambda s, c: (s, 0, 0, 0))
    o, s_new = pl.pallas_call(
        functools.partial(_rwkv_kernel, chunk=chunk, group=group),
        grid=(n_seq, n_chunks),
        in_specs=[tok] * 6 + [vec] * 5 + [state_spec],
        out_specs=[pl.BlockSpec((chunk, d), lambda s, c: (s * n_chunks + c, 0)), state_spec],
        out_shape=[jax.ShapeDtypeStruct((n_seq * seq_len, d), BF16),
                   jax.ShapeDtypeStruct(s_bd.shape, F32)],
        scratch_shapes=[pltpu.VMEM((n_groups, width, width), F32)],
        compiler_params=_params("parallel", "arbitrary"),
        name="rwkv_wkv",
    )(r, k, v, lw, a, g, k_k.reshape(1, d), k_a.reshape(1, d), r_k.reshape(1, d),
      lnx_g.reshape(1, d), lnx_b.reshape(1, d), s_bd)
    s6 = s_new.reshape(n_seq, n_groups, group, RWKV_HEAD_DIM, group, RWKV_HEAD_DIM)
    diag = jnp.stack([s6[:, :, h, :, h, :] for h in range(group)], axis=2)
    return o, diag.reshape(state0.shape)


def _forward(x_prompt, x_sample, cache_k, cache_v, page_table, state_gdn_conv, state_gdn,
             state_rwkv_shift, state_rwkv, p, *, tm, tm_proj, sb_tq, sb_tk, pages_per_step,
             chunk_prompt, gdn_chunk_sample, rwkv_chunk_sample, ffn_tf):
    bp, t_p, d = x_prompt.shape
    bs, t_s, _ = x_sample.shape
    assert bp == 1
    n_p, n_s = bp * t_p, bs * t_s
    x = jnp.concatenate([x_prompt.reshape(n_p, d), x_sample.reshape(n_s, d)], axis=0)

    w_in = p['w_in0']
    cut = 3 * SB_WIDTH + 3 * GDN_WIDTH
    w_pad = jnp.concatenate(
        [w_in[:, :cut], w_in[:, cut + 2 * GDN_HEADS:], w_in[:, cut:cut + 2 * GDN_HEADS],
         jnp.zeros((d, COL_BLOCK - 2 * GDN_HEADS), F32)], axis=1).astype(BF16)
    y = _matmul(x, w_pad, tm, COL_BLOCK)
    q = y[:, :SB_WIDTH]
    k_rows = y[:, SB_WIDTH:2 * SB_WIDTH]
    v_rows = y[:, 2 * SB_WIDTH:3 * SB_WIDTH]
    gdn_rows = y[:, 3 * SB_WIDTH:3 * SB_WIDTH + 3 * GDN_WIDTH]

    o_sb_p = _sb_prompt(q[:n_p], k_rows[:n_p], v_rows[:n_p], p['sb_bias'], sb_tq, sb_tk)
    shape_s = (bs, t_s, SB_WIDTH)
    o_sb_s = _sb_sample(q[n_p:].reshape(shape_s), k_rows[n_p:].reshape(shape_s), v_rows[n_p:].reshape(shape_s),
                        cache_k, cache_v, page_table, p['sb_bias'], pages_per_step)
    o_sb = jnp.concatenate([o_sb_p, o_sb_s.reshape(n_s, SB_WIDTH)], axis=0)

    gdn_args = (p['gdn_conv_w'], p['gdn_a_log'], p['gdn_dt_bias'], p['gdn_norm_w'])
    o_gdn_p, gdn_state_p = _gdn(y, 0, bp, t_p, chunk_prompt,
                                jnp.zeros((bp, GDN_CONV - 1, 3 * GDN_WIDTH), F32),
                                jnp.zeros((bp,) + state_gdn.shape[1:], F32), *gdn_args)
    o_gdn_s, gdn_state_s = _gdn(y, n_p, bs, t_s, gdn_chunk_sample, state_gdn_conv, state_gdn, *gdn_args)
    o_gdn = jnp.concatenate([o_gdn_p, o_gdn_s], axis=0)
    conv_p = gdn_rows[:n_p].reshape(bp, t_p, -1)[:, t_p - (GDN_CONV - 1):]
    conv_s = gdn_rows[n_p:].reshape(bs, t_s, -1)[:, t_s - (GDN_CONV - 1):]

    w_out = p['w_out0'].astype(BF16)
    x = _proj_ln([o_sb, o_gdn], [w_out[:SB_WIDTH], w_out[SB_WIDTH:]], x,
                 p['ln_gamma'][0, 0], p['ln_beta'][0, 0], tm, "out_proj_ln")
    x = _ffn_ln(x, p['ffn_gate'].astype(BF16)[None], p['ffn_up'].astype(BF16)[None],
                p['ffn_down'].astype(BF16)[None], p['ln_gamma'][0, 1], p['ln_beta'][0, 1], None,
                tm, ffn_tf, "ffn_ln")

    x_p = x[:n_p].reshape(bp, t_p, d)
    x_s = x[n_p:].reshape(bs, t_s, d)
    prev_p = jnp.concatenate([jnp.zeros((bp, 1, d), F32), x_p[:, :-1]], axis=1)
    prev_s = jnp.concatenate([state_rwkv_shift[:, None, :], x_s[:, :-1]], axis=1)
    x_prev = jnp.concatenate([prev_p.reshape(n_p, d), prev_s.reshape(n_s, d)], axis=0)
    bf = lambda name: p[name].astype(BF16)
    r, k, v, lw, a, g = _rwkv_proj(
        x, x_prev, p['rwkv_mix'], bf('rwkv_w_r'), bf('rwkv_w_k'), bf('rwkv_w_v'), bf('rwkv_w1'), bf('rwkv_w2'),
        bf('rwkv_a1'), bf('rwkv_a2'), bf('rwkv_g1'), bf('rwkv_g2'),
        p['rwkv_w0'].reshape(1, d), p['rwkv_a0'].reshape(1, d), tm_proj)
    rwkv_args = (p['rwkv_k_k'], p['rwkv_k_a'], p['rwkv_r_k'], p['rwkv_lnx_g'], p['rwkv_lnx_b'])
    o_p, rwkv_state_p = _rwkv(r, k, v, lw, a, g, 0, bp, t_p, chunk_prompt,
                              jnp.zeros((bp,) + state_rwkv.shape[1:], F32), *rwkv_args)
    o_s, rwkv_state_s = _rwkv(r, k, v, lw, a, g, n_p, bs, t_s, rwkv_chunk_sample, state_rwkv, *rwkv_args)
    x1 = _proj_ln([jnp.concatenate([o_p, o_s], axis=0)], [bf('rwkv_w_o')], x,
                  p['ln_gamma'][1, 0], p['ln_beta'][1, 0], tm, "rwkv_out_ln")
    w_router = jnp.pad(p['moe_router'], ((0, 0), (0, 128 - N_EXPERTS)))
    out = _ffn_ln(x1, bf('moe_gate'), bf('moe_up'), bf('moe_down'), p['ln_gamma'][1, 1], p['ln_beta'][1, 1],
                  w_router, tm, ffn_tf, "moe_ln")

    heads = lambda rows, b_, t_: rows.reshape(b_, t_, SB_HEADS, SB_HEAD_DIM)
    return (out[:n_p].reshape(bp, t_p, d), out[n_p:].reshape(bs, t_s, d),
            heads(k_rows[:n_p], bp, t_p), heads(v_rows[:n_p], bp, t_p), conv_p, gdn_state_p,
            x_p[:, -1], rwkv_state_p,
            heads(k_rows[n_p:], bs, t_s), heads(v_rows[n_p:], bs, t_s), conv_s, gdn_state_s,
            x_s[:, -1], rwkv_state_s)


def kernel(x_prompt, x_sample, cache_k, cache_v, page_table, state_gdn_conv, state_gdn, state_rwkv_shift, state_rwkv, w_in0, sb_bias, gdn_conv_w, gdn_a_log, gdn_dt_bias, gdn_norm_w, w_out0, ffn_gate, ffn_up, ffn_down, rwkv_mix, rwkv_w_r, rwkv_w_k, rwkv_w_v, rwkv_w0, rwkv_w1, rwkv_w2, rwkv_a0, rwkv_a1, rwkv_a2, rwkv_g1, rwkv_g2, rwkv_k_k, rwkv_k_a, rwkv_r_k, rwkv_lnx_g, rwkv_lnx_b, rwkv_w_o, moe_router, moe_gate, moe_up, moe_down, ln_gamma, ln_beta):
    p = dict(w_in0=w_in0, sb_bias=sb_bias, gdn_conv_w=gdn_conv_w, gdn_a_log=gdn_a_log, gdn_dt_bias=gdn_dt_bias,
             gdn_norm_w=gdn_norm_w, w_out0=w_out0, ffn_gate=ffn_gate, ffn_up=ffn_up, ffn_down=ffn_down,
             rwkv_mix=rwkv_mix, rwkv_w_r=rwkv_w_r, rwkv_w_k=rwkv_w_k, rwkv_w_v=rwkv_w_v,
             rwkv_w0=rwkv_w0, rwkv_w1=rwkv_w1, rwkv_w2=rwkv_w2, rwkv_a0=rwkv_a0, rwkv_a1=rwkv_a1,
             rwkv_a2=rwkv_a2, rwkv_g1=rwkv_g1, rwkv_g2=rwkv_g2, rwkv_k_k=rwkv_k_k, rwkv_k_a=rwkv_k_a,
             rwkv_r_k=rwkv_r_k, rwkv_lnx_g=rwkv_lnx_g, rwkv_lnx_b=rwkv_lnx_b, rwkv_w_o=rwkv_w_o,
             moe_router=moe_router, moe_gate=moe_gate, moe_up=moe_up, moe_down=moe_down,
             ln_gamma=ln_gamma, ln_beta=ln_beta)
    return _forward(x_prompt, x_sample, cache_k, cache_v, page_table, state_gdn_conv, state_gdn,
                    state_rwkv_shift, state_rwkv, p, tm=640, tm_proj=320, sb_tq=512, sb_tk=256,
                    pages_per_step=4, chunk_prompt=64, gdn_chunk_sample=8, rwkv_chunk_sample=8, ffn_tf=1408)
```

```python
import functools
import math

import jax
import jax.numpy as jnp
import numpy as np
from jax import lax
from jax.experimental import pallas as pl
from jax.experimental.pallas import tpu as pltpu

F32 = jnp.float32
BF16 = jnp.bfloat16

D_MODEL = 1024
SB_HEADS = 8
SB_HEAD_DIM = 64
SB_WIDTH = SB_HEADS * SB_HEAD_DIM
GDN_HEADS = 4
GDN_HEAD_DIM = 128
GDN_WIDTH = GDN_HEADS * GDN_HEAD_DIM
GDN_CONV = 4
RWKV_HEAD_DIM = 64
RWKV_HEADS = D_MODEL // RWKV_HEAD_DIM
RWKV_GN_EPS = 64e-5
N_EXPERTS = 8
DEPTH = 2
DEEPNORM_ALPHA = (2 * DEPTH) ** 0.25
LN_EPS = 1e-5
NORM_EPS = 1e-6
LOG2E = math.log2(math.e)

IN0_COLS = 4096
COL_BLOCK = 512
VMEM_LIMIT_BYTES = 56 * 1024 * 1024

_NT = (((1,), (1,)), ((), ()))
_TN = (((0,), (0,)), ((), ()))


def _params(*sem):
    return pltpu.CompilerParams(dimension_semantics=sem, vmem_limit_bytes=VMEM_LIMIT_BYTES)


def _dot(a, b):
    return jnp.dot(a.astype(BF16), b.astype(BF16), preferred_element_type=F32)


def _dot_nt(a, b):
    return lax.dot_general(a.astype(BF16), b.astype(BF16), _NT, preferred_element_type=F32)


def _dot_tn(a, b):
    return lax.dot_general(a.astype(BF16), b.astype(BF16), _TN, preferred_element_type=F32)


def _dot_exact_lhs(a01, x):
    a = a01.astype(BF16)
    x1 = x.astype(BF16)
    r1 = x - x1.astype(F32)
    x2 = r1.astype(BF16)
    x3 = (r1 - x2.astype(F32)).astype(BF16)
    out = jnp.dot(a, x1, preferred_element_type=F32)
    out = out + jnp.dot(a, x2, preferred_element_type=F32)
    return out + jnp.dot(a, x3, preferred_element_type=F32)


def _dot_exact_rhs(x, b01):
    b = b01.astype(BF16)
    x1 = x.astype(BF16)
    r1 = x - x1.astype(F32)
    x2 = r1.astype(BF16)
    x3 = (r1 - x2.astype(F32)).astype(BF16)
    out = jnp.dot(x1, b, preferred_element_type=F32)
    out = out + jnp.dot(x2, b, preferred_element_type=F32)
    return out + jnp.dot(x3, b, preferred_element_type=F32)


def _softplus(z):
    return jnp.maximum(z, 0.0) + jnp.log1p(jnp.exp(-jnp.abs(z)))


def _sigmoid(z):
    return 1.0 / (1.0 + jnp.exp(-z))


def _silu(z):
    return z * _sigmoid(z)


def _layer_norm(x, g, b):
    mu = jnp.mean(x, axis=-1, keepdims=True)
    xc = x - mu
    var = jnp.mean(xc * xc, axis=-1, keepdims=True)
    return xc * lax.rsqrt(var + LN_EPS) * g + b


def _inv_i_minus(n, size, block):
    return _inv_i_minus_many([n], size, block)[0]


def _inv_i_minus_many(ns, size, block):
    rows = lax.broadcasted_iota(jnp.int32, (size, size), 0)
    cols = lax.broadcasted_iota(jnp.int32, (size, size), 1)
    eye = jnp.where(rows == cols, 1.0, 0.0)
    ps = [eye + n for n in ns]
    ys = list(ns)
    for _ in range(max(0, int(math.ceil(math.log2(block))) - 1)):
        ys = [_dot(y, y) for y in ys]
        ps = [p + _dot(p, y) for p, y in zip(ps, ys)]
    return ps


def _matmul_kernel(x_ref, w_ref, o_ref):
    o_ref[...] = jnp.dot(x_ref[...].astype(BF16), w_ref[...], preferred_element_type=F32)


def _matmul(x, w, tm, tn):
    n, k = x.shape
    m = w.shape[1]
    return pl.pallas_call(
        _matmul_kernel,
        grid=(n // tm, m // tn),
        in_specs=[pl.BlockSpec((tm, k), lambda i, j: (i, 0)),
                  pl.BlockSpec((k, tn), lambda i, j: (0, j))],
        out_specs=pl.BlockSpec((tm, tn), lambda i, j: (i, j)),
        out_shape=jax.ShapeDtypeStruct((n, m), F32),
        compiler_params=_params("parallel", "parallel"),
        name="in_proj",
    )(x, w)


def _proj_ln_kernel(*refs, n_in):
    a_refs = refs[:n_in]
    w_refs = refs[n_in:2 * n_in]
    x_ref, g_ref, b_ref, o_ref = refs[2 * n_in:]
    h = jnp.dot(a_refs[0][...], w_refs[0][...], preferred_element_type=F32)
    for a_ref, w_ref in zip(a_refs[1:], w_refs[1:]):
        h = h + jnp.dot(a_ref[...], w_ref[...], preferred_element_type=F32)
    o_ref[...] = _layer_norm(DEEPNORM_ALPHA * x_ref[...] + h, g_ref[...], b_ref[...])


def _proj_ln(acts, weights, x, gamma, beta, tm, name):
    n, d = x.shape
    n_in = len(acts)
    in_specs = [pl.BlockSpec((tm, a.shape[1]), lambda i: (i, 0)) for a in acts]
    in_specs += [pl.BlockSpec(w.shape, lambda i: (0, 0)) for w in weights]
    in_specs += [pl.BlockSpec((tm, d), lambda i: (i, 0)),
                 pl.BlockSpec((1, d), lambda i: (0, 0)),
                 pl.BlockSpec((1, d), lambda i: (0, 0))]
    return pl.pallas_call(
        functools.partial(_proj_ln_kernel, n_in=n_in),
        grid=(n // tm,),
        in_specs=in_specs,
        out_specs=pl.BlockSpec((tm, d), lambda i: (i, 0)),
        out_shape=jax.ShapeDtypeStruct((n, d), F32),
        compiler_params=_params("parallel"),
        name=name,
    )(*acts, *weights, x, gamma.reshape(1, d), beta.reshape(1, d))


def _top2_gates(logits):
    lane = lax.broadcasted_iota(jnp.int32, logits.shape, 1).astype(F32)
    big = float(logits.shape[1])
    lg = jnp.where(lane < N_EXPERTS, logits, -jnp.inf)
    m1 = jnp.max(lg, axis=-1, keepdims=True)
    i1 = jnp.min(jnp.where(lg == m1, lane, big), axis=-1, keepdims=True)
    lg2 = jnp.where(lane == i1, -jnp.inf, lg)
    m2 = jnp.max(lg2, axis=-1, keepdims=True)
    i2 = jnp.min(jnp.where(lg2 == m2, lane, big), axis=-1, keepdims=True)
    e2 = jnp.exp(m2 - m1)
    den = 1.0 + e2
    gates = jnp.where(lane == i1, 1.0 / den, 0.0) + jnp.where(lane == i2, e2 / den, 0.0)
    chosen = jnp.where(jnp.logical_or(lane == i1, lane == i2), 1.0, 0.0)
    return gates, chosen


def _ffn_ln_kernel(x_ref, wg_ref, wu_ref, wd_ref, g_ref, b_ref, o_ref, acc_ref):
    f = pl.program_id(1)

    @pl.when(f == 0)
    def _():
        acc_ref[...] = jnp.zeros_like(acc_ref)

    xb = x_ref[...].astype(BF16)
    hg = jnp.dot(xb, wg_ref[...], preferred_element_type=F32)
    hu = jnp.dot(xb, wu_ref[...], preferred_element_type=F32)
    acc_ref[...] += jnp.dot((_silu(hg) * hu).astype(BF16), wd_ref[...], preferred_element_type=F32)

    @pl.when(f == pl.num_programs(1) - 1)
    def _():
        o_ref[...] = _layer_norm(DEEPNORM_ALPHA * x_ref[...] + acc_ref[...], g_ref[...], b_ref[...])


def _ffn_ln(x, w_gate, w_up, w_down, gamma, beta, tm, tf):
    n, d = x.shape
    d_ff = w_gate.shape[1]
    return pl.pallas_call(
        _ffn_ln_kernel,
        grid=(n // tm, d_ff // tf),
        in_specs=[pl.BlockSpec((tm, d), lambda i, f: (i, 0)),
                  pl.BlockSpec((d, tf), lambda i, f: (0, f)),
                  pl.BlockSpec((d, tf), lambda i, f: (0, f)),
                  pl.BlockSpec((tf, d), lambda i, f: (f, 0)),
                  pl.BlockSpec((1, d), lambda i, f: (0, 0)),
                  pl.BlockSpec((1, d), lambda i, f: (0, 0))],
        out_specs=pl.BlockSpec((tm, d), lambda i, f: (i, 0)),
        out_shape=jax.ShapeDtypeStruct((n, d), F32),
        scratch_shapes=[pltpu.VMEM((tm, d), F32)],
        compiler_params=_params("parallel", "arbitrary"),
        name="ffn_ln",
    )(x, w_gate, w_up, w_down, gamma.reshape(1, d), beta.reshape(1, d))


def _moe_kernel(x_ref, wg_ref, wu_ref, wd_ref, g_ref, b_ref, wr_ref, o_ref,
                xb_ref, gate_ref, key_ref, keyt_ref, xe_ref, ye_ref, *, rows, strip):
    tm, d = x_ref.shape
    e = pl.program_id(1)
    f = pl.program_id(2)
    n_f = pl.num_programs(2)
    n_lane = gate_ref.shape[1]

    @pl.when(jnp.logical_and(e == 0, f == 0))
    def _():
        x = x_ref[...]
        xb_ref[...] = x.astype(BF16)
        logits = jnp.dot(x, wr_ref[...], preferred_element_type=F32, precision=lax.Precision.HIGHEST)
        gates, chosen = _top2_gates(logits)
        gate_ref[...] = gates
        tr = lax.broadcasted_iota(jnp.int32, (tm, tm), 0)
        tc = lax.broadcasted_iota(jnp.int32, (tm, tm), 1)
        before = jnp.where(tc < tr, 1.0, 0.0).astype(BF16)
        rank = jnp.dot(before, chosen.astype(BF16), preferred_element_type=F32)
        key = jnp.where(chosen > 0.0, rank, -1.0)
        key_ref[...] = key
        keyt_ref[...] = key.T
        o_ref[...] = jnp.zeros_like(o_ref)

    lane = lax.broadcasted_iota(jnp.int32, (tm, n_lane), 1)
    key_col = jnp.sum(jnp.where(lane == e, key_ref[...], 0.0), axis=-1, keepdims=True)
    count = jnp.sum(jnp.where(key_col >= 0.0, 1.0, 0.0)).astype(jnp.int32)
    n_chunks = (count + rows - 1) // rows

    @pl.when(f == 0)
    def _():
        key_row = keyt_ref[pl.ds(e, 1), :]

        def gather(c, carry):
            slot = (c * rows + lax.broadcasted_iota(jnp.int32, (rows, tm), 0)).astype(F32)
            sel = jnp.where(key_row == slot, 1.0, 0.0).astype(BF16)
            start = pl.multiple_of(c * rows, rows)
            xe_ref[pl.ds(start, rows), :] = jnp.dot(sel, xb_ref[...], preferred_element_type=F32).astype(BF16)
            return carry

        lax.fori_loop(0, n_chunks, gather, 0)

    def expert(c, carry):
        start = pl.multiple_of(c * rows, rows)
        xe = xe_ref[pl.ds(start, rows), :]
        hg = jnp.dot(xe, wg_ref[0], preferred_element_type=F32)
        hu = jnp.dot(xe, wu_ref[0], preferred_element_type=F32)
        y = jnp.dot((_silu(hg) * hu).astype(BF16), wd_ref[0], preferred_element_type=F32)

        @pl.when(f == 0)
        def _():
            ye_ref[pl.ds(start, rows), :] = y

        @pl.when(f > 0)
        def _():
            ye_ref[pl.ds(start, rows), :] += y

        return carry

    lax.fori_loop(0, n_chunks, expert, 0)

    @pl.when(f == n_f - 1)
    def _():
        gate_col = jnp.sum(jnp.where(lane == e, gate_ref[...], 0.0), axis=-1, keepdims=True)

        def scatter(c, carry):
            start = pl.multiple_of(c * rows, rows)
            y = ye_ref[pl.ds(start, rows), :]
            y_hi = y.astype(BF16)
            y_lo = (y - y_hi.astype(F32)).astype(BF16)
            for s in range(tm // strip):
                tok = slice(s * strip, (s + 1) * strip)
                slot = (c * rows + lax.broadcasted_iota(jnp.int32, (strip, rows), 1)).astype(F32)
                sel = jnp.where(key_col[tok] == slot, 1.0, 0.0).astype(BF16)
                back = (jnp.dot(sel, y_hi, preferred_element_type=F32)
                        + jnp.dot(sel, y_lo, preferred_element_type=F32))
                o_ref[tok, :] += gate_col[tok] * back
            return carry

        lax.fori_loop(0, n_chunks, scatter, 0)

    @pl.when(jnp.logical_and(e == pl.num_programs(1) - 1, f == n_f - 1))
    def _():
        for s in range(tm // strip):
            tok = slice(s * strip, (s + 1) * strip)
            o_ref[tok, :] = _layer_norm(DEEPNORM_ALPHA * x_ref[tok, :] + o_ref[tok, :], g_ref[...], b_ref[...])


def _moe_ln(x, w_gate, w_up, w_down, gamma, beta, w_router, tm, tf, rows, strip):
    n, d = x.shape
    n_e, _, d_ff = w_gate.shape
    assert tm % strip == 0
    cap = -(-tm // rows) * rows
    once = dict(pipeline_mode=pl.Buffered(1))
    return pl.pallas_call(
        functools.partial(_moe_kernel, rows=rows, strip=strip),
        grid=(n // tm, n_e, d_ff // tf),
        in_specs=[pl.BlockSpec((tm, d), lambda i, e, f: (i, 0), **once),
                  pl.BlockSpec((1, d, tf), lambda i, e, f: (e, 0, f)),
                  pl.BlockSpec((1, d, tf), lambda i, e, f: (e, 0, f)),
                  pl.BlockSpec((1, tf, d), lambda i, e, f: (e, f, 0)),
                  pl.BlockSpec((1, d), lambda i, e, f: (0, 0)),
                  pl.BlockSpec((1, d), lambda i, e, f: (0, 0)),
                  pl.BlockSpec(w_router.shape, lambda i, e, f: (0, 0))],
        out_specs=pl.BlockSpec((tm, d), lambda i, e, f: (i, 0)),
        out_shape=jax.ShapeDtypeStruct((n, d), F32),
        scratch_shapes=[pltpu.VMEM((tm, d), BF16),
                        pltpu.VMEM((tm, w_router.shape[1]), F32),
                        pltpu.VMEM((tm, w_router.shape[1]), F32),
                        pltpu.VMEM((w_router.shape[1], tm), F32),
                        pltpu.VMEM((cap, d), BF16),
                        pltpu.VMEM((cap, d), F32)],
        compiler_params=_params("parallel", "arbitrary", "arbitrary"),
        name="moe_ln",
    )(x, w_gate, w_up, w_down, gamma.reshape(1, d), beta.reshape(1, d), w_router)


def _rwkv_proj_kernel(x_ref, xp_ref, mix_ref, wr_ref, wk_ref, wv_ref, w1_ref, w2_ref, a1_ref, a2_ref,
                      g1_ref, g2_ref, w0_ref, a0_ref, r_ref, k_ref, v_ref, lw_ref, a_ref, g_ref):
    x = x_ref[...]
    xx = xp_ref[...] - x
    mixed = lambda i: (x + xx * mix_ref[i:i + 1, :]).astype(BF16)
    r_ref[...] = jnp.dot(mixed(0), wr_ref[...], preferred_element_type=F32)
    w_lora = _dot(jnp.tanh(jnp.dot(mixed(1), w1_ref[...], preferred_element_type=F32)), w2_ref[...])
    lw_ref[...] = -jnp.exp(-_softplus(-(w0_ref[...] + w_lora)) - 0.5)
    k_ref[...] = jnp.dot(mixed(2), wk_ref[...], preferred_element_type=F32)
    v_ref[...] = jnp.dot(mixed(3), wv_ref[...], preferred_element_type=F32)
    a_lora = _dot(jnp.dot(mixed(4), a1_ref[...], preferred_element_type=F32), a2_ref[...])
    a_ref[...] = _sigmoid(a0_ref[...] + a_lora)
    g_ref[...] = _dot(_sigmoid(jnp.dot(mixed(5), g1_ref[...], preferred_element_type=F32)), g2_ref[...])


def _rwkv_proj(x, x_prev, mix, w_r, w_k, w_v, w1, w2, a1, a2, g1, g2, w0, a0, tm):
    n, d = x.shape
    tok = pl.BlockSpec((tm, d), lambda i: (i, 0))
    full = lambda a: pl.BlockSpec(a.shape, lambda i: (0, 0))
    consts = [mix, w_r, w_k, w_v, w1, w2, a1, a2, g1, g2, w0, a0]
    return pl.pallas_call(
        _rwkv_proj_kernel,
        grid=(n // tm,),
        in_specs=[tok, tok] + [full(c) for c in consts],
        out_specs=[tok] * 6,
        out_shape=[jax.ShapeDtypeStruct((n, d), F32)] * 6,
        compiler_params=_params("parallel"),
        name="rwkv_proj",
    )(x, x_prev, *consts)


def _suffix_sum_rows8(x):
    row = lax.broadcasted_iota(jnp.int32, x.shape, 0)
    for sh in (1, 2, 4):
        shifted = pltpu.roll(x, 8 - sh, axis=0)
        x = x + jnp.where(row < 8 - sh, shifted, 0.0)
    return x


def _sb_block(z_ref, r_ref, a_ref, carry, visible):
    tk, tq = z_ref.shape
    nsub = tk // 8
    run = jnp.zeros((8, tq), F32)
    for s in reversed(range(nsub)):
        rows = slice(s * 8, (s + 1) * 8)
        z = z_ref[rows, :]
        neg_abs = pltpu.bitcast(pltpu.bitcast(z, jnp.uint32) | jnp.uint32(0x80000000), F32)
        sp = jnp.maximum(z, 0.0) + jnp.log2(1.0 + jnp.exp2(neg_abs))
        vis = visible(s * 8, 8)
        if vis is None:
            run = run + sp
            r_ref[rows, :] = z - run
        else:
            r_ref[rows, :] = (z - sp) - run
            run = run + jnp.where(vis, sp, 0.0)
    incl = _suffix_sum_rows8(run)
    offset = incl - run + carry
    offset2 = jnp.concatenate([offset, offset], axis=0)
    for s in range(nsub // 2):
        rows = slice(s * 16, (s + 1) * 16)
        a = jnp.exp2(r_ref[rows, :] - offset2)
        vis = visible(s * 16, 16)
        if vis is not None:
            a = jnp.where(vis, a, 0.0)
        a_ref[rows, :] = a.astype(BF16)
    return carry + incl[0:1]


def _sb_prompt_kernel(q_ref, k_ref, vt_ref, o_ref, z0_ref, z1_ref, a0_ref, a1_ref, *, tq, tk):
    assert tq == 2 * tk
    i = pl.program_id(1)
    q = q_ref[0]
    nsub = tk // 8

    def scores(j):
        start = pl.multiple_of(jnp.maximum(j, 0) * tk, tk)
        return lax.dot_general(k_ref[0, pl.ds(start, tk), :], q, _NT, preferred_element_type=F32)

    def weighted_v(j, a_ref):
        start = pl.multiple_of(j * tk, tk)
        return jnp.dot(vt_ref[0, :, pl.ds(start, tk)], a_ref[...], preferred_element_type=F32)

    def weights(z_ref, a_ref, j, carry, masked):
        def visible(first_row, n_rows):
            if not masked:
                return None
            row = first_row + lax.broadcasted_iota(jnp.int32, (n_rows, tq), 0)
            col = lax.broadcasted_iota(jnp.int32, (n_rows, tq), 1)
            k_pos = j * tk + (row % 8) * nsub + row // 8
            return k_pos < i * tq + col
        return _sb_block(z_ref, z_ref, a_ref, carry, visible)

    newest = 2 * i + 1
    acc = jnp.zeros((SB_HEAD_DIM, tq), F32)
    carry = jnp.zeros((1, tq), F32)
    z0_ref[...] = scores(newest)
    z1_ref[...] = scores(newest - 1)
    carry = weights(z0_ref, a0_ref, newest, carry, True)
    z0_ref[...] = scores(newest - 2)
    acc = acc + weighted_v(newest, a0_ref)
    carry = weights(z1_ref, a1_ref, newest - 1, carry, True)

    def body(t, state):
        acc, carry = state
        blk = newest - 2 - 2 * t
        z1_ref[...] = scores(blk - 1)
        acc = acc + weighted_v(blk + 1, a1_ref)
        carry = weights(z0_ref, a0_ref, blk, carry, False)
        z0_ref[...] = scores(blk - 2)
        acc = acc + weighted_v(blk, a0_ref)
        carry = weights(z1_ref, a1_ref, blk - 1, carry, False)
        return acc, carry

    acc, carry = lax.fori_loop(0, i, body, (acc, carry))
    acc = acc + weighted_v(0, a1_ref)
    o_ref[0] = acc.astype(o_ref.dtype)


def _bf16_pieces(x, n):
    out = []
    for _ in range(n):
        piece = x.astype(BF16)
        out.append(piece)
        x = x - piece.astype(F32)
    return out


def _sb_prompt(q, k, v, bias, tq, tk):
    t = q.shape[0]
    nblk, nsub = t // tk, tk // 8
    pad = 128 - SB_HEAD_DIM
    qh = (q * (SB_HEAD_DIM ** -0.5 * LOG2E)).astype(BF16).reshape(t, SB_HEADS, SB_HEAD_DIM).transpose(1, 0, 2)
    bias_cols = jnp.stack(_bf16_pieces(bias.astype(F32) * LOG2E, 3), axis=-1)
    q_extra = jnp.pad(bias_cols, ((0, 0), (0, pad - 3)))[:, None, :]
    q_aug = jnp.concatenate([qh, jnp.broadcast_to(q_extra, (SB_HEADS, t, pad))], axis=-1)
    k5 = k.astype(BF16).reshape(nblk, 8, nsub, SB_HEADS, SB_HEAD_DIM)
    kh = k5.transpose(3, 0, 2, 1, 4).reshape(SB_HEADS, t, SB_HEAD_DIM)
    k_extra = jnp.pad(jnp.ones((3,), BF16), (0, pad - 3))
    k_aug = jnp.concatenate([kh, jnp.broadcast_to(k_extra, (SB_HEADS, t, pad))], axis=-1)
    v5 = v.astype(BF16).reshape(nblk, 8, nsub, SB_HEADS, SB_HEAD_DIM)
    vt = v5.transpose(3, 4, 0, 2, 1).reshape(SB_HEADS, SB_HEAD_DIM, t)
    out_t = pl.pallas_call(
        functools.partial(_sb_prompt_kernel, tq=tq, tk=tk),
        grid=(SB_HEADS, t // tq),
        in_specs=[pl.BlockSpec((1, tq, 128), lambda h, i: (h, i, 0)),
                  pl.BlockSpec((1, t, 128), lambda h, i: (h, 0, 0)),
                  pl.BlockSpec((1, SB_HEAD_DIM, t), lambda h, i: (h, 0, 0))],
        out_specs=pl.BlockSpec((1, SB_HEAD_DIM, tq), lambda h, i: (h, 0, i)),
        out_shape=jax.ShapeDtypeStruct((SB_HEADS, SB_HEAD_DIM, t), BF16),
        scratch_shapes=[pltpu.VMEM((tk, tq), F32), pltpu.VMEM((tk, tq), F32),
                        pltpu.VMEM((tk, tq), BF16), pltpu.VMEM((tk, tq), BF16)],
        compiler_params=_params("parallel", "parallel"),
        name="sb_prompt",
    )(q_aug, k_aug, vt)
    return out_t.transpose(2, 0, 1).reshape(t, SB_WIDTH)


def _sb_sample_kernel(pt_ref, qbd_ref, bias_ref, kn_ref, vn_ref, *rest, pages_per_step, n_q):
    del pt_ref
    k_refs = rest[:pages_per_step]
    v_refs = rest[pages_per_step:2 * pages_per_step]
    o_ref, acc_ref, carry_ref = rest[2 * pages_per_step:]
    g = pl.program_id(1)
    qbd = qbd_ref[0]
    bias = bias_ref[...]
    page = k_refs[0].shape[3]
    kj = lax.broadcasted_iota(jnp.int32, (page, 2 * page), 0)
    ks = lax.broadcasted_iota(jnp.int32, (page, 2 * page), 1)
    later_or_all = jnp.where(jnp.logical_or(ks >= page, kj > ks), 1.0, 0.0)

    @pl.when(g == 0)
    def _():
        z = _dot_nt(qbd, kn_ref[0]) + bias
        t = lax.broadcasted_iota(jnp.int32, z.shape, 0) % n_q
        s = lax.broadcasted_iota(jnp.int32, z.shape, 1)
        vis = s < t
        sp = _softplus(z)
        sums = _dot_exact_rhs(jnp.where(vis, sp, 0.0), later_or_all)
        a = jnp.where(vis, jnp.exp(z - sp - sums[:, :page]), 0.0)
        acc_ref[...] = _dot(a, vn_ref[0])
        carry_ref[...] = sums[:, page:]

    acc = acc_ref[...]
    carry = carry_ref[...]
    flat = lambda ref: ref[0].reshape(SB_WIDTH, page)
    zs = [_dot(qbd, flat(k_ref)) + bias for k_ref in k_refs]
    sps = [_softplus(z) for z in zs]
    sums = [_dot_exact_rhs(sp, later_or_all) for sp in sps]
    for z, sp, sm, v_ref in zip(zs, sps, sums, v_refs):
        a = jnp.exp(z - sp - sm[:, :page] - carry)
        acc = acc + _dot_nt(a, flat(v_ref))
        carry = carry + sm[:, page:]
    acc_ref[...] = acc
    carry_ref[...] = carry

    @pl.when(g == pl.num_programs(1) - 1)
    def _():
        r = lax.broadcasted_iota(jnp.int32, acc.shape, 0)
        c = lax.broadcasted_iota(jnp.int32, acc.shape, 1)
        own = jnp.where(r // n_q == c // SB_HEAD_DIM, acc, 0.0)
        o_ref[0] = jnp.sum(own.reshape(SB_HEADS, n_q, SB_WIDTH), axis=0).astype(o_ref.dtype)


def _sb_sample(q, k_new, v_new, cache_k, cache_v, page_table, bias, pages_per_step):
    b, n_q, _ = q.shape
    n_pages = page_table.shape[1]
    page = cache_k.shape[1]
    ck = jnp.transpose(cache_k, (0, 2, 3, 1))
    cv = jnp.transpose(cache_v, (0, 2, 3, 1))
    q4 = (q * SB_HEAD_DIM ** -0.5).astype(BF16).reshape(b, n_q, SB_HEADS, SB_HEAD_DIM).transpose(0, 2, 1, 3)
    eye = jnp.eye(SB_HEADS, dtype=BF16)
    qbd = (q4[:, :, :, None, :] * eye[None, :, None, :, None]).reshape(b, SB_HEADS * n_q, SB_WIDTH)
    bias_col = jnp.repeat(bias.astype(F32), n_q).reshape(SB_HEADS * n_q, 1)
    pad_keys = lambda x: jnp.pad(x, ((0, 0), (0, page - n_q), (0, 0)))
    steps = n_pages // pages_per_step

    def page_map(u):
        return lambda s, g, pt: (pt[s, n_pages - 1 - (g * pages_per_step + u)], 0, 0, 0)

    page_specs = [pl.BlockSpec((1, SB_HEADS, SB_HEAD_DIM, page), page_map(u)) for u in range(pages_per_step)]
    per_seq = lambda shape: pl.BlockSpec((1,) + shape, lambda s, g, pt: (s, 0, 0))
    grid_spec = pltpu.PrefetchScalarGridSpec(
        num_scalar_prefetch=1,
        grid=(b, steps),
        in_specs=[per_seq((SB_HEADS * n_q, SB_WIDTH)),
                  pl.BlockSpec((SB_HEADS * n_q, 1), lambda s, g, pt: (0, 0)),
                  per_seq((page, SB_WIDTH)), per_seq((page, SB_WIDTH))] + page_specs + page_specs,
        out_specs=per_seq((n_q, SB_WIDTH)),
        scratch_shapes=[pltpu.VMEM((SB_HEADS * n_q, SB_WIDTH), F32),
                        pltpu.VMEM((SB_HEADS * n_q, page), F32)],
    )
    return pl.pallas_call(
        functools.partial(_sb_sample_kernel, pages_per_step=pages_per_step, n_q=n_q),
        grid_spec=grid_spec,
        out_shape=jax.ShapeDtypeStruct((b, n_q, SB_WIDTH), BF16),
        compiler_params=_params("parallel", "arbitrary"),
        name="sb_sample",
    )(page_table, qbd, bias_col, pad_keys(k_new), pad_keys(v_new),
      *([ck] * pages_per_step), *([cv] * pages_per_step))


def _gdn_kernel(hp_ref, qkv_ref, gate_ref, ab_ref, cbuf_ref, cw_ref, nw_ref, s0_ref,
                o_ref, s_out_ref, s_ref, carry_ref, *, chunk):
    c = pl.program_id(1)

    @pl.when(c == 0)
    def _():
        s_ref[...] = s0_ref[0]
        carry_ref[...] = cbuf_ref[0]

    x = qkv_ref[...]
    ext = jnp.concatenate([carry_ref[...], x], axis=0)
    conv = ext[5:5 + chunk] * cw_ref[0:1, :]
    for i in range(1, GDN_CONV):
        conv = conv + ext[5 + i:5 + i + chunk] * cw_ref[i:i + 1, :]
    carry_ref[...] = ext[chunk:chunk + 8]
    act = _silu(conv)

    rows = lax.broadcasted_iota(jnp.int32, (chunk, chunk), 0)
    cols = lax.broadcasted_iota(jnp.int32, (chunk, chunk), 1)
    lower = cols <= rows
    strict = cols < rows
    lower01 = jnp.where(lower, 1.0, 0.0)
    ab = ab_ref[...]
    d = GDN_HEAD_DIM
    heads = range(GDN_HEADS)
    head_cols = lambda x, base, h: x[:, base + h * d:base + (h + 1) * d]
    l2n = lambda x: x * lax.rsqrt(jnp.sum(x * x, axis=-1, keepdims=True) + NORM_EPS)
    qn = [l2n(head_cols(act, 0, h)) * (d ** -0.5) for h in heads]
    kn = [l2n(head_cols(act, GDN_WIDTH, h)) for h in heads]
    vh = [head_cols(act, 2 * GDN_WIDTH, h) for h in heads]
    beta = [_sigmoid(ab[:, GDN_HEADS + h:GDN_HEADS + h + 1]) for h in heads]
    g = [-jnp.exp(hp_ref[0:1, h:h + 1]) * _softplus(ab[:, h:h + 1] + hp_ref[1:2, h:h + 1]) for h in heads]
    gc = [_dot_exact_lhs(lower01, jnp.broadcast_to(g[h], (chunk, d))) for h in heads]
    gi = [gc[h][:, :chunk] if chunk <= d else jnp.broadcast_to(gc[h][:, :1], (chunk, chunk)) for h in heads]
    decay = [jnp.exp(jnp.where(lower, gi[h] - gi[h].T, -jnp.inf)) for h in heads]
    kb = [kn[h] * beta[h] for h in heads]
    tri = [jnp.where(strict, _dot_nt(kb[h], kn[h]) * decay[h], 0.0) for h in heads]
    attn = [jnp.where(lower, _dot_nt(qn[h], kn[h]) * decay[h], 0.0) for h in heads]
    t_inv = _inv_i_minus_many([-t for t in tri], chunk, chunk)
    e_gc = [jnp.exp(gc[h]) for h in heads]
    uw = [_dot(t_inv[h], jnp.concatenate([vh[h] * beta[h], kb[h] * e_gc[h]], axis=-1)) for h in heads]
    s = [s_ref[h] for h in heads]
    v_new = [uw[h][:, :d] - _dot_nt(uw[h][:, d:], s[h]) for h in heads]
    o = [_dot_nt(qn[h] * e_gc[h], s[h]) + _dot(attn[h], v_new[h]) for h in heads]
    for h in heads:
        g_last = gc[h][chunk - 1:chunk, :]
        s_ref[h] = s[h] * jnp.exp(g_last) + _dot_tn(v_new[h], kn[h] * jnp.exp(g_last - gc[h]))
    for h in heads:
        on = o[h] * lax.rsqrt(jnp.mean(o[h] * o[h], axis=-1, keepdims=True) + NORM_EPS) * nw_ref[...]
        o_ref[:, h * d:(h + 1) * d] = (on * _silu(gate_ref[:, h * d:(h + 1) * d])).astype(o_ref.dtype)

    @pl.when(c == pl.num_programs(1) - 1)
    def _():
        s_out_ref[0] = s_ref[...]


def _gdn(y_all, row0, n_seq, seq_len, chunk, conv_buf, state0, conv_w, a_log, dt_bias, norm_w):
    n_chunks = seq_len // chunk
    blk0 = row0 // chunk
    row_map = lambda col: (lambda s, c: (blk0 + s * n_chunks + c, col))
    cbuf = jnp.pad(conv_buf, ((0, 0), (8 - (GDN_CONV - 1), 0), (0, 0)))
    cw = jnp.pad(conv_w, ((0, 8 - GDN_CONV), (0, 0)))
    head_params = jnp.zeros((8, 128), F32).at[0, :GDN_HEADS].set(a_log).at[1, :GDN_HEADS].set(dt_bias)
    state_spec = pl.BlockSpec((1, GDN_HEADS, GDN_HEAD_DIM, GDN_HEAD_DIM), lambda s, c: (s, 0, 0, 0))
    o, s_new = pl.pallas_call(
        functools.partial(_gdn_kernel, chunk=chunk),
        grid=(n_seq, n_chunks),
        in_specs=[pl.BlockSpec((8, 128), lambda s, c: (0, 0)),
                  pl.BlockSpec((chunk, 3 * GDN_WIDTH), row_map(1)),
                  pl.BlockSpec((chunk, COL_BLOCK), row_map(6)),
                  pl.BlockSpec((chunk, COL_BLOCK), row_map(7)),
                  pl.BlockSpec((1, 8, 3 * GDN_WIDTH), lambda s, c: (s, 0, 0)),
                  pl.BlockSpec((8, 3 * GDN_WIDTH), lambda s, c: (0, 0)),
                  pl.BlockSpec((1, GDN_HEAD_DIM), lambda s, c: (0, 0)),
                  state_spec],
        out_specs=[pl.BlockSpec((chunk, GDN_WIDTH), lambda s, c: (s * n_chunks + c, 0)), state_spec],
        out_shape=[jax.ShapeDtypeStruct((n_seq * seq_len, GDN_WIDTH), BF16),
                   jax.ShapeDtypeStruct(state0.shape, F32)],
        scratch_shapes=[pltpu.VMEM((GDN_HEADS, GDN_HEAD_DIM, GDN_HEAD_DIM), F32),
                        pltpu.VMEM((8, 3 * GDN_WIDTH), F32)],
        compiler_params=_params("parallel", "arbitrary"),
        name="gdn",
    )(head_params, y_all, y_all, y_all, cbuf, cw, norm_w.reshape(1, GDN_HEAD_DIM),
      jnp.swapaxes(state0, -1, -2))
    return o, jnp.swapaxes(s_new, -1, -2)


def _rwkv_kernel(r_ref, k_ref, v_ref, lw_ref, a_ref, g_ref, kk_ref, ka_ref, rk_ref, lg_ref, lb_ref,
                 s0_ref, o_ref, s_out_ref, s_ref, *, chunk, group):
    c = pl.program_id(1)

    @pl.when(c == 0)
    def _():
        s_ref[...] = s0_ref[0]

    hd = RWKV_HEAD_DIM
    width = group * hd
    size = group * chunk
    rows = lax.broadcasted_iota(jnp.int32, (size, size), 0)
    cols = lax.broadcasted_iota(jnp.int32, (size, size), 1)
    same = rows // chunk == cols // chunk
    lower = jnp.logical_and(same, cols <= rows)
    strict = jnp.logical_and(same, cols < rows)
    r2 = lax.broadcasted_iota(jnp.int32, (size, width), 0)
    c2 = lax.broadcasted_iota(jnp.int32, (size, width), 1)
    own = r2 // chunk == c2 // hd
    groups = range(D_MODEL // width)
    cols_of = lambda gi: slice(gi * width, (gi + 1) * width)
    stack = lambda x, gi: jnp.where(own, jnp.concatenate([x[:, cols_of(gi)]] * group, axis=0), 0.0)

    tr = lax.broadcasted_iota(jnp.int32, (chunk, chunk), 0)
    tc = lax.broadcasted_iota(jnp.int32, (chunk, chunk), 1)
    lw_all = lw_ref[...]
    cw_all = _dot_exact_lhs(jnp.where(tc <= tr, 1.0, 0.0), lw_all)
    tot_all = jnp.sum(lw_all, axis=0, keepdims=True)

    r = [stack(r_ref[...], gi) for gi in groups]
    v = [stack(v_ref[...], gi) for gi in groups]
    a = [stack(a_ref[...], gi) for gi in groups]
    lw = [stack(lw_all, gi) for gi in groups]
    cw = [stack(cw_all, gi) for gi in groups]
    k_in = [stack(k_ref[...], gi) for gi in groups]
    kk = [k_in[gi] * kk_ref[:, cols_of(gi)] for gi in groups]
    kk = [x * lax.rsqrt(jnp.sum(x * x, axis=-1, keepdims=True) + NORM_EPS) for x in kk]
    k = [k_in[gi] * (1.0 + (a[gi] - 1.0) * ka_ref[:, cols_of(gi)]) for gi in groups]
    b = [kk[gi] * a[gi] for gi in groups]
    tot = [tot_all[:, cols_of(gi)] for gi in groups]
    e_neg = [jnp.exp(-cw[gi]) for gi in groups]
    e_end = [jnp.exp(jnp.where(own, tot[gi] - cw[gi], 0.0)) for gi in groups]
    at = [-kk[gi] * jnp.exp(cw[gi] - lw[gi]) for gi in groups]
    rt = [r[gi] * jnp.exp(cw[gi]) for gi in groups]
    bt = [b[gi] * e_neg[gi] for gi in groups]
    kt = [k[gi] * e_neg[gi] for gi in groups]
    a_ab = [jnp.where(strict, _dot_nt(at[gi], bt[gi]), 0.0) for gi in groups]
    a_ak = [jnp.where(strict, _dot_nt(at[gi], kt[gi]), 0.0) for gi in groups]
    a_rb = [jnp.where(lower, _dot_nt(rt[gi], bt[gi]), 0.0) for gi in groups]
    a_rk = [jnp.where(lower, _dot_nt(rt[gi], kt[gi]), 0.0) for gi in groups]
    t_inv = _inv_i_minus_many(a_ab, size, chunk)
    s = [s_ref[gi] for gi in groups]
    rhs = [_dot_nt(at[gi], s[gi]) + _dot(a_ak[gi], v[gi]) for gi in groups]
    o_past = [_dot_nt(rt[gi], s[gi]) + _dot(a_rk[gi], v[gi]) for gi in groups]
    u = [_dot(t_inv[gi], rhs[gi]) for gi in groups]
    o = [o_past[gi] + _dot(a_rb[gi], u[gi]) for gi in groups]
    for gi in groups:
        s_ref[gi] = (s[gi] * jnp.exp(tot[gi]) + _dot_tn(u[gi], b[gi] * e_end[gi])
                     + _dot_tn(v[gi], k[gi] * e_end[gi]))
    for gi in groups:
        sl = cols_of(gi)
        mu = jnp.sum(o[gi], axis=-1, keepdims=True) * (1.0 / hd)
        oc = jnp.where(own, o[gi] - mu, 0.0)
        var = jnp.sum(oc * oc, axis=-1, keepdims=True) * (1.0 / hd)
        on = oc * lax.rsqrt(var + RWKV_GN_EPS) * lg_ref[:, sl] + jnp.where(own, lb_ref[:, sl], 0.0)
        on = on + jnp.sum(r[gi] * k[gi] * rk_ref[:, sl], axis=-1, keepdims=True) * v[gi]
        out = on[0:chunk]
        for hh in range(1, group):
            out = out + on[hh * chunk:(hh + 1) * chunk]
        o_ref[:, sl] = (out * g_ref[:, sl]).astype(o_ref.dtype)

    @pl.when(c == pl.num_programs(1) - 1)
    def _():
        s_out_ref[0] = s_ref[...]


def _rwkv(r, k, v, lw, a, g, row0, n_seq, seq_len, chunk, state0, k_k, k_a, r_k, lnx_g, lnx_b, group=4):
    d = D_MODEL
    n_groups = RWKV_HEADS // group
    width = group * RWKV_HEAD_DIM
    n_chunks = seq_len // chunk
    blk0 = row0 // chunk
    tok = pl.BlockSpec((chunk, d), lambda s, c: (blk0 + s * n_chunks + c, 0))
    vec = pl.BlockSpec((1, d), lambda s, c: (0, 0))
    s5 = state0.reshape(n_seq, n_groups, group, RWKV_HEAD_DIM, RWKV_HEAD_DIM)
    eye = jnp.eye(group, dtype=F32)
    s_bd = (s5[:, :, :, :, None, :] * eye[None, None, :, None, :, None]).reshape(n_seq, n_groups, width, width)
    state_spec = pl.BlockSpec((1, n_groups, width, width), lambda s, c: (s, 0, 0, 0))
    o, s_new = pl.pallas_call(
        functools.partial(_rwkv_kernel, chunk=chunk, group=group),
        grid=(n_seq, n_chunks),
        in_specs=[tok] * 6 + [vec] * 5 + [state_spec],
        out_specs=[pl.BlockSpec((chunk, d), lambda s, c: (s * n_chunks + c, 0)), state_spec],
        out_shape=[jax.ShapeDtypeStruct((n_seq * seq_len, d), BF16),
                   jax.ShapeDtypeStruct(s_bd.shape, F32)],
        scratch_shapes=[pltpu.VMEM((n_groups, width, width), F32)],
        compiler_params=_params("parallel", "arbitrary"),
        name="rwkv_wkv",
    )(r, k, v, lw, a, g, k_k.reshape(1, d), k_a.reshape(1, d), r_k.reshape(1, d),
      lnx_g.reshape(1, d), lnx_b.reshape(1, d), s_bd)
    s6 = s_new.reshape(n_seq, n_groups, group, RWKV_HEAD_DIM, group, RWKV_HEAD_DIM)
    diag = jnp.stack([s6[:, :, h, :, h, :] for h in range(group)], axis=2)
    return o, diag.reshape(state0.shape)


def _forward(x_prompt, x_sample, cache_k, cache_v, page_table, state_gdn_conv, state_gdn,
             state_rwkv_shift, state_rwkv, p, *, tm, tm_proj, sb_tq, sb_tk, pages_per_step,
             chunk_prompt, gdn_chunk_sample, rwkv_chunk_sample, ffn_tf, tm_moe, moe_rows, moe_strip):
    bp, t_p, d = x_prompt.shape
    bs, t_s, _ = x_sample.shape
    assert bp == 1
    n_p, n_s = bp * t_p, bs * t_s
    x = jnp.concatenate([x_prompt.reshape(n_p, d), x_sample.reshape(n_s, d)], axis=0)

    w_in = p['w_in0']
    cut = 3 * SB_WIDTH + 3 * GDN_WIDTH
    w_pad = jnp.concatenate(
        [w_in[:, :cut], w_in[:, cut + 2 * GDN_HEADS:], w_in[:, cut:cut + 2 * GDN_HEADS],
         jnp.zeros((d, COL_BLOCK - 2 * GDN_HEADS), F32)], axis=1).astype(BF16)
    y = _matmul(x, w_pad, tm, COL_BLOCK)
    q = y[:, :SB_WIDTH]
    k_rows = y[:, SB_WIDTH:2 * SB_WIDTH]
    v_rows = y[:, 2 * SB_WIDTH:3 * SB_WIDTH]
    gdn_rows = y[:, 3 * SB_WIDTH:3 * SB_WIDTH + 3 * GDN_WIDTH]

    o_sb_p = _sb_prompt(q[:n_p], k_rows[:n_p], v_rows[:n_p], p['sb_bias'], sb_tq, sb_tk)
    shape_s = (bs, t_s, SB_WIDTH)
    o_sb_s = _sb_sample(q[n_p:].reshape(shape_s), k_rows[n_p:].reshape(shape_s), v_rows[n_p:].reshape(shape_s),
                        cache_k, cache_v, page_table, p['sb_bias'], pages_per_step)
    o_sb = jnp.concatenate([o_sb_p, o_sb_s.reshape(n_s, SB_WIDTH)], axis=0)

    gdn_args = (p['gdn_conv_w'], p['gdn_a_log'], p['gdn_dt_bias'], p['gdn_norm_w'])
    o_gdn_p, gdn_state_p = _gdn(y, 0, bp, t_p, chunk_prompt,
                                jnp.zeros((bp, GDN_CONV - 1, 3 * GDN_WIDTH), F32),
                                jnp.zeros((bp,) + state_gdn.shape[1:], F32), *gdn_args)
    o_gdn_s, gdn_state_s = _gdn(y, n_p, bs, t_s, gdn_chunk_sample, state_gdn_conv, state_gdn, *gdn_args)
    o_gdn = jnp.concatenate([o_gdn_p, o_gdn_s], axis=0)
    conv_p = gdn_rows[:n_p].reshape(bp, t_p, -1)[:, t_p - (GDN_CONV - 1):]
    conv_s = gdn_rows[n_p:].reshape(bs, t_s, -1)[:, t_s - (GDN_CONV - 1):]

    w_out = p['w_out0'].astype(BF16)
    x = _proj_ln([o_sb, o_gdn], [w_out[:SB_WIDTH], w_out[SB_WIDTH:]], x,
                 p['ln_gamma'][0, 0], p['ln_beta'][0, 0], tm, "out_proj_ln")
    x = _ffn_ln(x, p['ffn_gate'].astype(BF16), p['ffn_up'].astype(BF16), p['ffn_down'].astype(BF16),
                p['ln_gamma'][0, 1], p['ln_beta'][0, 1], tm, ffn_tf)

    x_p = x[:n_p].reshape(bp, t_p, d)
    x_s = x[n_p:].reshape(bs, t_s, d)
    prev_p = jnp.concatenate([jnp.zeros((bp, 1, d), F32), x_p[:, :-1]], axis=1)
    prev_s = jnp.concatenate([state_rwkv_shift[:, None, :], x_s[:, :-1]], axis=1)
    x_prev = jnp.concatenate([prev_p.reshape(n_p, d), prev_s.reshape(n_s, d)], axis=0)
    bf = lambda name: p[name].astype(BF16)
    r, k, v, lw, a, g = _rwkv_proj(
        x, x_prev, p['rwkv_mix'], bf('rwkv_w_r'), bf('rwkv_w_k'), bf('rwkv_w_v'), bf('rwkv_w1'), bf('rwkv_w2'),
        bf('rwkv_a1'), bf('rwkv_a2'), bf('rwkv_g1'), bf('rwkv_g2'),
        p['rwkv_w0'].reshape(1, d), p['rwkv_a0'].reshape(1, d), tm_proj)
    rwkv_args = (p['rwkv_k_k'], p['rwkv_k_a'], p['rwkv_r_k'], p['rwkv_lnx_g'], p['rwkv_lnx_b'])
    o_p, rwkv_state_p = _rwkv(r, k, v, lw, a, g, 0, bp, t_p, chunk_prompt,
                              jnp.zeros((bp,) + state_rwkv.shape[1:], F32), *rwkv_args)
    o_s, rwkv_state_s = _rwkv(r, k, v, lw, a, g, n_p, bs, t_s, rwkv_chunk_sample, state_rwkv, *rwkv_args)
    x1 = _proj_ln([jnp.concatenate([o_p, o_s], axis=0)], [bf('rwkv_w_o')], x,
                  p['ln_gamma'][1, 0], p['ln_beta'][1, 0], tm, "rwkv_out_ln")
    w_router = jnp.pad(p['moe_router'], ((0, 0), (0, 128 - N_EXPERTS)))
    out = _moe_ln(x1, bf('moe_gate'), bf('moe_up'), bf('moe_down'), p['ln_gamma'][1, 1], p['ln_beta'][1, 1],
                  w_router, tm_moe, ffn_tf, moe_rows, moe_strip)

    heads = lambda rows, b_, t_: rows.reshape(b_, t_, SB_HEADS, SB_HEAD_DIM)
    return (out[:n_p].reshape(bp, t_p, d), out[n_p:].reshape(bs, t_s, d),
            heads(k_rows[:n_p], bp, t_p), heads(v_rows[:n_p], bp, t_p), conv_p, gdn_state_p,
            x_p[:, -1], rwkv_state_p,
            heads(k_rows[n_p:], bs, t_s), heads(v_rows[n_p:], bs, t_s), conv_s, gdn_state_s,
            x_s[:, -1], rwkv_state_s)


def kernel(x_prompt, x_sample, cache_k, cache_v, page_table, state_gdn_conv, state_gdn, state_rwkv_shift, state_rwkv, w_in0, sb_bias, gdn_conv_w, gdn_a_log, gdn_dt_bias, gdn_norm_w, w_out0, ffn_gate, ffn_up, ffn_down, rwkv_mix, rwkv_w_r, rwkv_w_k, rwkv_w_v, rwkv_w0, rwkv_w1, rwkv_w2, rwkv_a0, rwkv_a1, rwkv_a2, rwkv_g1, rwkv_g2, rwkv_k_k, rwkv_k_a, rwkv_r_k, rwkv_lnx_g, rwkv_lnx_b, rwkv_w_o, moe_router, moe_gate, moe_up, moe_down, ln_gamma, ln_beta):
    p = dict(w_in0=w_in0, sb_bias=sb_bias, gdn_conv_w=gdn_conv_w, gdn_a_log=gdn_a_log, gdn_dt_bias=gdn_dt_bias,
             gdn_norm_w=gdn_norm_w, w_out0=w_out0, ffn_gate=ffn_gate, ffn_up=ffn_up, ffn_down=ffn_down,
             rwkv_mix=rwkv_mix, rwkv_w_r=rwkv_w_r, rwkv_w_k=rwkv_w_k, rwkv_w_v=rwkv_w_v,
             rwkv_w0=rwkv_w0, rwkv_w1=rwkv_w1, rwkv_w2=rwkv_w2, rwkv_a0=rwkv_a0, rwkv_a1=rwkv_a1,
             rwkv_a2=rwkv_a2, rwkv_g1=rwkv_g1, rwkv_g2=rwkv_g2, rwkv_k_k=rwkv_k_k, rwkv_k_a=rwkv_k_a,
             rwkv_r_k=rwkv_r_k, rwkv_lnx_g=rwkv_lnx_g, rwkv_lnx_b=rwkv_lnx_b, rwkv_w_o=rwkv_w_o,
             moe_router=moe_router, moe_gate=moe_gate, moe_up=moe_up, moe_down=moe_down,
             ln_gamma=ln_gamma, ln_beta=ln_beta)
    return _forward(x_prompt, x_sample, cache_k, cache_v, page_table, state_gdn_conv, state_gdn,
                    state_rwkv_shift, state_rwkv, p, tm=640, tm_proj=320, sb_tq=512, sb_tk=256,
                    pages_per_step=8, chunk_prompt=64, gdn_chunk_sample=8, rwkv_chunk_sample=8, ffn_tf=1408,
                    tm_moe=1280, moe_rows=384, moe_strip=256)
```

```python
import functools
import math

import jax
import jax.numpy as jnp
import numpy as np
from jax import lax
from jax.experimental import pallas as pl
from jax.experimental.pallas import tpu as pltpu

F32 = jnp.float32
BF16 = jnp.bfloat16

D_MODEL = 1024
SB_HEADS = 8
SB_HEAD_DIM = 64
SB_WIDTH = SB_HEADS * SB_HEAD_DIM
GDN_HEADS = 4
GDN_HEAD_DIM = 128
GDN_WIDTH = GDN_HEADS * GDN_HEAD_DIM
GDN_CONV = 4
RWKV_HEAD_DIM = 64
RWKV_HEADS = D_MODEL // RWKV_HEAD_DIM
RWKV_GN_EPS = 64e-5
N_EXPERTS = 8
DEPTH = 2
DEEPNORM_ALPHA = (2 * DEPTH) ** 0.25
LN_EPS = 1e-5
NORM_EPS = 1e-6
LOG2E = math.log2(math.e)

IN0_COLS = 4096
COL_BLOCK = 512
VMEM_LIMIT_BYTES = 56 * 1024 * 1024

_NT = (((1,), (1,)), ((), ()))
_TN = (((0,), (0,)), ((), ()))


def _params(*sem):
    return pltpu.CompilerParams(dimension_semantics=sem, vmem_limit_bytes=VMEM_LIMIT_BYTES)


def _dot(a, b):
    return jnp.dot(a.astype(BF16), b.astype(BF16), preferred_element_type=F32)


def _dot_nt(a, b):
    return lax.dot_general(a.astype(BF16), b.astype(BF16), _NT, preferred_element_type=F32)


def _dot_tn(a, b):
    return lax.dot_general(a.astype(BF16), b.astype(BF16), _TN, preferred_element_type=F32)


def _dot_exact_lhs(a01, x):
    a = a01.astype(BF16)
    x1 = x.astype(BF16)
    r1 = x - x1.astype(F32)
    x2 = r1.astype(BF16)
    x3 = (r1 - x2.astype(F32)).astype(BF16)
    out = jnp.dot(a, x1, preferred_element_type=F32)
    out = out + jnp.dot(a, x2, preferred_element_type=F32)
    return out + jnp.dot(a, x3, preferred_element_type=F32)


def _dot_exact_rhs(x, b01):
    b = b01.astype(BF16)
    x1 = x.astype(BF16)
    r1 = x - x1.astype(F32)
    x2 = r1.astype(BF16)
    x3 = (r1 - x2.astype(F32)).astype(BF16)
    out = jnp.dot(x1, b, preferred_element_type=F32)
    out = out + jnp.dot(x2, b, preferred_element_type=F32)
    return out + jnp.dot(x3, b, preferred_element_type=F32)


def _softplus(z):
    return jnp.maximum(z, 0.0) + jnp.log1p(jnp.exp(-jnp.abs(z)))


def _sigmoid(z):
    return 1.0 / (1.0 + jnp.exp(-z))


def _silu(z):
    return z * _sigmoid(z)


def _layer_norm(x, g, b):
    mu = jnp.mean(x, axis=-1, keepdims=True)
    xc = x - mu
    var = jnp.mean(xc * xc, axis=-1, keepdims=True)
    return xc * lax.rsqrt(var + LN_EPS) * g + b


def _inv_i_minus(n, size, block):
    return _inv_i_minus_many([n], size, block)[0]


def _inv_i_minus_many(ns, size, block):
    rows = lax.broadcasted_iota(jnp.int32, (size, size), 0)
    cols = lax.broadcasted_iota(jnp.int32, (size, size), 1)
    eye = jnp.where(rows == cols, 1.0, 0.0)
    ps = [eye + n for n in ns]
    ys = list(ns)
    for _ in range(max(0, int(math.ceil(math.log2(block))) - 1)):
        ys = [_dot(y, y) for y in ys]
        ps = [p + _dot(p, y) for p, y in zip(ps, ys)]
    return ps


def _matmul_kernel(x_ref, w_ref, o_ref, *, tn):
    xb = x_ref[...].astype(BF16)
    for j in range(w_ref.shape[1] // tn):
        o_ref[:, j * tn:(j + 1) * tn] = jnp.dot(xb, w_ref[:, j * tn:(j + 1) * tn], preferred_element_type=F32)


def _matmul(x, w, tm, tn):
    n, k = x.shape
    m = w.shape[1]
    return pl.pallas_call(
        functools.partial(_matmul_kernel, tn=tn),
        grid=(n // tm,),
        in_specs=[pl.BlockSpec((tm, k), lambda i: (i, 0)),
                  pl.BlockSpec((k, m), lambda i: (0, 0))],
        out_specs=pl.BlockSpec((tm, m), lambda i: (i, 0)),
        out_shape=jax.ShapeDtypeStruct((n, m), F32),
        compiler_params=_params("parallel"),
        name="in_proj",
    )(x, w)


def _proj_ln_kernel(*refs, n_in):
    a_refs = refs[:n_in]
    w_refs = refs[n_in:2 * n_in]
    x_ref, g_ref, b_ref, o_ref = refs[2 * n_in:]
    h = jnp.dot(a_refs[0][...], w_refs[0][...], preferred_element_type=F32)
    for a_ref, w_ref in zip(a_refs[1:], w_refs[1:]):
        h = h + jnp.dot(a_ref[...], w_ref[...], preferred_element_type=F32)
    o_ref[...] = _layer_norm(DEEPNORM_ALPHA * x_ref[...] + h, g_ref[...], b_ref[...])


def _proj_ln(acts, weights, x, gamma, beta, tm, name):
    n, d = x.shape
    n_in = len(acts)
    in_specs = [pl.BlockSpec((tm, a.shape[1]), lambda i: (i, 0)) for a in acts]
    in_specs += [pl.BlockSpec(w.shape, lambda i: (0, 0)) for w in weights]
    in_specs += [pl.BlockSpec((tm, d), lambda i: (i, 0)),
                 pl.BlockSpec((1, d), lambda i: (0, 0)),
                 pl.BlockSpec((1, d), lambda i: (0, 0))]
    return pl.pallas_call(
        functools.partial(_proj_ln_kernel, n_in=n_in),
        grid=(n // tm,),
        in_specs=in_specs,
        out_specs=pl.BlockSpec((tm, d), lambda i: (i, 0)),
        out_shape=jax.ShapeDtypeStruct((n, d), F32),
        compiler_params=_params("parallel"),
        name=name,
    )(*acts, *weights, x, gamma.reshape(1, d), beta.reshape(1, d))


def _top2_gates(logits):
    lane = lax.broadcasted_iota(jnp.int32, logits.shape, 1).astype(F32)
    big = float(logits.shape[1])
    lg = jnp.where(lane < N_EXPERTS, logits, -jnp.inf)
    m1 = jnp.max(lg, axis=-1, keepdims=True)
    i1 = jnp.min(jnp.where(lg == m1, lane, big), axis=-1, keepdims=True)
    lg2 = jnp.where(lane == i1, -jnp.inf, lg)
    m2 = jnp.max(lg2, axis=-1, keepdims=True)
    i2 = jnp.min(jnp.where(lg2 == m2, lane, big), axis=-1, keepdims=True)
    e2 = jnp.exp(m2 - m1)
    den = 1.0 + e2
    gates = jnp.where(lane == i1, 1.0 / den, 0.0) + jnp.where(lane == i2, e2 / den, 0.0)
    chosen = jnp.where(jnp.logical_or(lane == i1, lane == i2), 1.0, 0.0)
    return gates, chosen


def _ffn_ln_kernel(x_ref, wg_ref, wu_ref, wd_ref, g_ref, b_ref, o_ref, acc_ref):
    f = pl.program_id(1)

    @pl.when(f == 0)
    def _():
        acc_ref[...] = jnp.zeros_like(acc_ref)

    xb = x_ref[...].astype(BF16)
    hg = jnp.dot(xb, wg_ref[...], preferred_element_type=F32)
    hu = jnp.dot(xb, wu_ref[...], preferred_element_type=F32)
    acc_ref[...] += jnp.dot((_silu(hg) * hu).astype(BF16), wd_ref[...], preferred_element_type=F32)

    @pl.when(f == pl.num_programs(1) - 1)
    def _():
        o_ref[...] = _layer_norm(DEEPNORM_ALPHA * x_ref[...] + acc_ref[...], g_ref[...], b_ref[...])


def _ffn_ln(x, w_gate, w_up, w_down, gamma, beta, tm, tf):
    n, d = x.shape
    d_ff = w_gate.shape[1]
    return pl.pallas_call(
        _ffn_ln_kernel,
        grid=(n // tm, d_ff // tf),
        in_specs=[pl.BlockSpec((tm, d), lambda i, f: (i, 0)),
                  pl.BlockSpec((d, tf), lambda i, f: (0, f)),
                  pl.BlockSpec((d, tf), lambda i, f: (0, f)),
                  pl.BlockSpec((tf, d), lambda i, f: (f, 0)),
                  pl.BlockSpec((1, d), lambda i, f: (0, 0)),
                  pl.BlockSpec((1, d), lambda i, f: (0, 0))],
        out_specs=pl.BlockSpec((tm, d), lambda i, f: (i, 0)),
        out_shape=jax.ShapeDtypeStruct((n, d), F32),
        scratch_shapes=[pltpu.VMEM((tm, d), F32)],
        compiler_params=_params("parallel", "arbitrary"),
        name="ffn_ln",
    )(x, w_gate, w_up, w_down, gamma.reshape(1, d), beta.reshape(1, d))


def _moe_kernel(x_ref, wg_ref, wu_ref, wd_ref, g_ref, b_ref, wr_ref, o_ref,
                xb_ref, gate_ref, key_ref, keyt_ref, xe_ref, ye_ref, *, rows, strip):
    tm, d = x_ref.shape
    e = pl.program_id(1)
    f = pl.program_id(2)
    n_f = pl.num_programs(2)
    n_lane = gate_ref.shape[1]

    @pl.when(jnp.logical_and(e == 0, f == 0))
    def _():
        x = x_ref[...]
        xb_ref[...] = x.astype(BF16)
        logits = jnp.dot(x, wr_ref[...], preferred_element_type=F32, precision=lax.Precision.HIGHEST)
        gates, chosen = _top2_gates(logits)
        gate_ref[...] = gates
        tr = lax.broadcasted_iota(jnp.int32, (tm, tm), 0)
        tc = lax.broadcasted_iota(jnp.int32, (tm, tm), 1)
        before = jnp.where(tc < tr, 1.0, 0.0).astype(BF16)
        rank = jnp.dot(before, chosen.astype(BF16), preferred_element_type=F32)
        key = jnp.where(chosen > 0.0, rank, -1.0)
        key_ref[...] = key
        keyt_ref[...] = key.T
        o_ref[...] = jnp.zeros_like(o_ref)

    lane = lax.broadcasted_iota(jnp.int32, (tm, n_lane), 1)
    key_col = jnp.sum(jnp.where(lane == e, key_ref[...], 0.0), axis=-1, keepdims=True)
    count = jnp.sum(jnp.where(key_col >= 0.0, 1.0, 0.0)).astype(jnp.int32)
    n_chunks = (count + rows - 1) // rows

    @pl.when(f == 0)
    def _():
        key_row = keyt_ref[pl.ds(e, 1), :]

        def gather(c, carry):
            slot = (c * rows + lax.broadcasted_iota(jnp.int32, (rows, tm), 0)).astype(F32)
            sel = jnp.where(key_row == slot, 1.0, 0.0).astype(BF16)
            start = pl.multiple_of(c * rows, rows)
            xe_ref[pl.ds(start, rows), :] = jnp.dot(sel, xb_ref[...], preferred_element_type=F32).astype(BF16)
            return carry

        lax.fori_loop(0, n_chunks, gather, 0)

    def expert(c, carry):
        start = pl.multiple_of(c * rows, rows)
        xe = xe_ref[pl.ds(start, rows), :]
        hg = jnp.dot(xe, wg_ref[0], preferred_element_type=F32)
        hu = jnp.dot(xe, wu_ref[0], preferred_element_type=F32)
        y = jnp.dot((_silu(hg) * hu).astype(BF16), wd_ref[0], preferred_element_type=F32)

        @pl.when(f == 0)
        def _():
            ye_ref[pl.ds(start, rows), :] = y

        @pl.when(f > 0)
        def _():
            ye_ref[pl.ds(start, rows), :] += y

        return carry

    lax.fori_loop(0, n_chunks, expert, 0)

    @pl.when(f == n_f - 1)
    def _():
        gate_col = jnp.sum(jnp.where(lane == e, gate_ref[...], 0.0), axis=-1, keepdims=True)

        def scatter(c, carry):
            start = pl.multiple_of(c * rows, rows)
            y = ye_ref[pl.ds(start, rows), :].astype(BF16)
            for s in range(tm // strip):
                tok = slice(s * strip, (s + 1) * strip)
                slot = (c * rows + lax.broadcasted_iota(jnp.int32, (strip, rows), 1)).astype(F32)
                sel = jnp.where(key_col[tok] == slot, 1.0, 0.0).astype(BF16)
                o_ref[tok, :] += gate_col[tok] * jnp.dot(sel, y, preferred_element_type=F32)
            return carry

        lax.fori_loop(0, n_chunks, scatter, 0)

    @pl.when(jnp.logical_and(e == pl.num_programs(1) - 1, f == n_f - 1))
    def _():
        for s in range(tm // strip):
            tok = slice(s * strip, (s + 1) * strip)
            o_ref[tok, :] = _layer_norm(DEEPNORM_ALPHA * x_ref[tok, :] + o_ref[tok, :], g_ref[...], b_ref[...])


def _moe_ln(x, w_gate, w_up, w_down, gamma, beta, w_router, tm, tf, rows, strip):
    n, d = x.shape
    n_e, _, d_ff = w_gate.shape
    assert tm % strip == 0
    cap = -(-tm // rows) * rows
    once = dict(pipeline_mode=pl.Buffered(1))
    return pl.pallas_call(
        functools.partial(_moe_kernel, rows=rows, strip=strip),
        grid=(n // tm, n_e, d_ff // tf),
        in_specs=[pl.BlockSpec((tm, d), lambda i, e, f: (i, 0), **once),
                  pl.BlockSpec((1, d, tf), lambda i, e, f: (e, 0, f)),
                  pl.BlockSpec((1, d, tf), lambda i, e, f: (e, 0, f)),
                  pl.BlockSpec((1, tf, d), lambda i, e, f: (e, f, 0)),
                  pl.BlockSpec((1, d), lambda i, e, f: (0, 0)),
                  pl.BlockSpec((1, d), lambda i, e, f: (0, 0)),
                  pl.BlockSpec(w_router.shape, lambda i, e, f: (0, 0))],
        out_specs=pl.BlockSpec((tm, d), lambda i, e, f: (i, 0)),
        out_shape=jax.ShapeDtypeStruct((n, d), F32),
        scratch_shapes=[pltpu.VMEM((tm, d), BF16),
                        pltpu.VMEM((tm, w_router.shape[1]), F32),
                        pltpu.VMEM((tm, w_router.shape[1]), F32),
                        pltpu.VMEM((w_router.shape[1], tm), F32),
                        pltpu.VMEM((cap, d), BF16),
                        pltpu.VMEM((cap, d), F32)],
        compiler_params=_params("parallel", "arbitrary", "arbitrary"),
        name="moe_ln",
    )(x, w_gate, w_up, w_down, gamma.reshape(1, d), beta.reshape(1, d), w_router)


def _rwkv_proj_kernel(x_ref, xp_ref, mix_ref, wr_ref, wk_ref, wv_ref, w1_ref, w2_ref, a1_ref, a2_ref,
                      g1_ref, g2_ref, w0_ref, a0_ref, r_ref, k_ref, v_ref, lw_ref, a_ref, g_ref):
    x = x_ref[...]
    xx = xp_ref[...] - x
    mixed = lambda i: (x + xx * mix_ref[i:i + 1, :]).astype(BF16)
    r_ref[...] = jnp.dot(mixed(0), wr_ref[...], preferred_element_type=F32)
    w_lora = _dot(jnp.tanh(jnp.dot(mixed(1), w1_ref[...], preferred_element_type=F32)), w2_ref[...])
    lw_ref[...] = -jnp.exp(-_softplus(-(w0_ref[...] + w_lora)) - 0.5)
    k_ref[...] = jnp.dot(mixed(2), wk_ref[...], preferred_element_type=F32)
    v_ref[...] = jnp.dot(mixed(3), wv_ref[...], preferred_element_type=F32)
    a_lora = _dot(jnp.dot(mixed(4), a1_ref[...], preferred_element_type=F32), a2_ref[...])
    a_ref[...] = _sigmoid(a0_ref[...] + a_lora)
    g_ref[...] = _dot(_sigmoid(jnp.dot(mixed(5), g1_ref[...], preferred_element_type=F32)), g2_ref[...])


def _rwkv_proj(x, x_prev, mix, w_r, w_k, w_v, w1, w2, a1, a2, g1, g2, w0, a0, tm):
    n, d = x.shape
    tok = pl.BlockSpec((tm, d), lambda i: (i, 0))
    full = lambda a: pl.BlockSpec(a.shape, lambda i: (0, 0))
    consts = [mix, w_r, w_k, w_v, w1, w2, a1, a2, g1, g2, w0, a0]
    return pl.pallas_call(
        _rwkv_proj_kernel,
        grid=(n // tm,),
        in_specs=[tok, tok] + [full(c) for c in consts],
        out_specs=[tok] * 6,
        out_shape=[jax.ShapeDtypeStruct((n, d), F32)] * 6,
        compiler_params=_params("parallel"),
        name="rwkv_proj",
    )(x, x_prev, *consts)


def _suffix_sum_rows8(x):
    row = lax.broadcasted_iota(jnp.int32, x.shape, 0)
    for sh in (1, 2, 4):
        shifted = pltpu.roll(x, 8 - sh, axis=0)
        x = x + jnp.where(row < 8 - sh, shifted, 0.0)
    return x


def _sb_block(z_ref, r_ref, a_ref, carry, visible):
    tk, tq = z_ref.shape
    nsub = tk // 8
    run = jnp.zeros((8, tq), F32)
    for s in reversed(range(nsub)):
        rows = slice(s * 8, (s + 1) * 8)
        z = z_ref[rows, :]
        neg_abs = pltpu.bitcast(pltpu.bitcast(z, jnp.uint32) | jnp.uint32(0x80000000), F32)
        sp = jnp.maximum(z, 0.0) + jnp.log2(1.0 + jnp.exp2(neg_abs))
        vis = visible(s * 8, 8)
        if vis is None:
            run = run + sp
            r_ref[rows, :] = z - run
        else:
            r_ref[rows, :] = (z - sp) - run
            run = run + jnp.where(vis, sp, 0.0)
    incl = _suffix_sum_rows8(run)
    offset = incl - run + carry
    offset2 = jnp.concatenate([offset, offset], axis=0)
    for s in range(nsub // 2):
        rows = slice(s * 16, (s + 1) * 16)
        a = jnp.exp2(r_ref[rows, :] - offset2)
        vis = visible(s * 16, 16)
        if vis is not None:
            a = jnp.where(vis, a, 0.0)
        a_ref[rows, :] = a.astype(BF16)
    return carry + incl[0:1]


def _sb_prompt_kernel(q_ref, k_ref, vt_ref, o_ref, z0_ref, z1_ref, a0_ref, a1_ref, *, tq, tk):
    assert tq == 2 * tk
    i = pl.program_id(1)
    q = q_ref[0]
    nsub = tk // 8

    def scores(j):
        start = pl.multiple_of(jnp.maximum(j, 0) * tk, tk)
        return lax.dot_general(k_ref[0, pl.ds(start, tk), :], q, _NT, preferred_element_type=F32)

    def weighted_v(j, a_ref):
        start = pl.multiple_of(j * tk, tk)
        return jnp.dot(vt_ref[0, :, pl.ds(start, tk)], a_ref[...], preferred_element_type=F32)

    def weights(z_ref, a_ref, j, carry, masked):
        def visible(first_row, n_rows):
            if not masked:
                return None
            row = first_row + lax.broadcasted_iota(jnp.int32, (n_rows, tq), 0)
            col = lax.broadcasted_iota(jnp.int32, (n_rows, tq), 1)
            k_pos = j * tk + (row % 8) * nsub + row // 8
            return k_pos < i * tq + col
        return _sb_block(z_ref, z_ref, a_ref, carry, visible)

    newest = 2 * i + 1
    acc = jnp.zeros((SB_HEAD_DIM, tq), F32)
    carry = jnp.zeros((1, tq), F32)
    z0_ref[...] = scores(newest)
    z1_ref[...] = scores(newest - 1)
    carry = weights(z0_ref, a0_ref, newest, carry, True)
    z0_ref[...] = scores(newest - 2)
    acc = acc + weighted_v(newest, a0_ref)
    carry = weights(z1_ref, a1_ref, newest - 1, carry, True)

    def body(t, state):
        acc, carry = state
        blk = newest - 2 - 2 * t
        z1_ref[...] = scores(blk - 1)
        acc = acc + weighted_v(blk + 1, a1_ref)
        carry = weights(z0_ref, a0_ref, blk, carry, False)
        z0_ref[...] = scores(blk - 2)
        acc = acc + weighted_v(blk, a0_ref)
        carry = weights(z1_ref, a1_ref, blk - 1, carry, False)
        return acc, carry

    acc, carry = lax.fori_loop(0, i, body, (acc, carry))
    acc = acc + weighted_v(0, a1_ref)
    o_ref[0] = acc.astype(o_ref.dtype)


def _bf16_pieces(x, n):
    out = []
    for _ in range(n):
        piece = x.astype(BF16)
        out.append(piece)
        x = x - piece.astype(F32)
    return out


def _sb_prompt(q, k, v, bias, tq, tk):
    t = q.shape[0]
    nblk, nsub = t // tk, tk // 8
    pad = 128 - SB_HEAD_DIM
    qh = (q * (SB_HEAD_DIM ** -0.5 * LOG2E)).astype(BF16).reshape(t, SB_HEADS, SB_HEAD_DIM).transpose(1, 0, 2)
    bias_cols = jnp.stack(_bf16_pieces(bias.astype(F32) * LOG2E, 3), axis=-1)
    q_extra = jnp.pad(bias_cols, ((0, 0), (0, pad - 3)))[:, None, :]
    q_aug = jnp.concatenate([qh, jnp.broadcast_to(q_extra, (SB_HEADS, t, pad))], axis=-1)
    k5 = k.astype(BF16).reshape(nblk, 8, nsub, SB_HEADS, SB_HEAD_DIM)
    kh = k5.transpose(3, 0, 2, 1, 4).reshape(SB_HEADS, t, SB_HEAD_DIM)
    k_extra = jnp.pad(jnp.ones((3,), BF16), (0, pad - 3))
    k_aug = jnp.concatenate([kh, jnp.broadcast_to(k_extra, (SB_HEADS, t, pad))], axis=-1)
    v5 = v.astype(BF16).reshape(nblk, 8, nsub, SB_HEADS, SB_HEAD_DIM)
    vt = v5.transpose(3, 4, 0, 2, 1).reshape(SB_HEADS, SB_HEAD_DIM, t)
    out_t = pl.pallas_call(
        functools.partial(_sb_prompt_kernel, tq=tq, tk=tk),
        grid=(SB_HEADS, t // tq),
        in_specs=[pl.BlockSpec((1, tq, 128), lambda h, i: (h, i, 0)),
                  pl.BlockSpec((1, t, 128), lambda h, i: (h, 0, 0)),
                  pl.BlockSpec((1, SB_HEAD_DIM, t), lambda h, i: (h, 0, 0))],
        out_specs=pl.BlockSpec((1, SB_HEAD_DIM, tq), lambda h, i: (h, 0, i)),
        out_shape=jax.ShapeDtypeStruct((SB_HEADS, SB_HEAD_DIM, t), BF16),
        scratch_shapes=[pltpu.VMEM((tk, tq), F32), pltpu.VMEM((tk, tq), F32),
                        pltpu.VMEM((tk, tq), BF16), pltpu.VMEM((tk, tq), BF16)],
        compiler_params=_params("parallel", "parallel"),
        name="sb_prompt",
    )(q_aug, k_aug, vt)
    return out_t.transpose(2, 0, 1).reshape(t, SB_WIDTH)


def _sb_sample_kernel(pt_ref, qbd_ref, bias_ref, kn_ref, vn_ref, *rest, pages_per_step, n_q):
    del pt_ref
    k_refs = rest[:pages_per_step]
    v_refs = rest[pages_per_step:2 * pages_per_step]
    o_ref, acc_ref, carry_ref = rest[2 * pages_per_step:]
    g = pl.program_id(1)
    qbd = qbd_ref[0]
    bias = bias_ref[...]
    page = k_refs[0].shape[3]
    kj = lax.broadcasted_iota(jnp.int32, (page, 2 * page), 0)
    ks = lax.broadcasted_iota(jnp.int32, (page, 2 * page), 1)
    later_or_all = jnp.where(jnp.logical_or(ks >= page, kj > ks), 1.0, 0.0)

    @pl.when(g == 0)
    def _():
        z = _dot_nt(qbd, kn_ref[0]) + bias
        t = lax.broadcasted_iota(jnp.int32, z.shape, 0) % n_q
        s = lax.broadcasted_iota(jnp.int32, z.shape, 1)
        vis = s < t
        sp = _softplus(z)
        sums = _dot_exact_rhs(jnp.where(vis, sp, 0.0), later_or_all)
        a = jnp.where(vis, jnp.exp(z - sp - sums[:, :page]), 0.0)
        acc_ref[...] = _dot(a, vn_ref[0])
        carry_ref[...] = sums[:, page:]

    acc = acc_ref[...]
    carry = carry_ref[...]
    flat = lambda ref: ref[0].reshape(SB_WIDTH, page)
    zs = [_dot(qbd, flat(k_ref)) + bias for k_ref in k_refs]
    sps = [_softplus(z) for z in zs]
    sums = [_dot_exact_rhs(sp, later_or_all) for sp in sps]
    for z, sp, sm, v_ref in zip(zs, sps, sums, v_refs):
        a = jnp.exp(z - sp - sm[:, :page] - carry)
        acc = acc + _dot_nt(a, flat(v_ref))
        carry = carry + sm[:, page:]
    acc_ref[...] = acc
    carry_ref[...] = carry

    @pl.when(g == pl.num_programs(1) - 1)
    def _():
        r = lax.broadcasted_iota(jnp.int32, acc.shape, 0)
        c = lax.broadcasted_iota(jnp.int32, acc.shape, 1)
        own = jnp.where(r // n_q == c // SB_HEAD_DIM, acc, 0.0)
        o_ref[0] = jnp.sum(own.reshape(SB_HEADS, n_q, SB_WIDTH), axis=0).astype(o_ref.dtype)


def _sb_sample(q, k_new, v_new, cache_k, cache_v, page_table, bias, pages_per_step):
    b, n_q, _ = q.shape
    n_pages = page_table.shape[1]
    page = cache_k.shape[1]
    ck = jnp.transpose(cache_k, (0, 2, 3, 1))
    cv = jnp.transpose(cache_v, (0, 2, 3, 1))
    q4 = (q * SB_HEAD_DIM ** -0.5).astype(BF16).reshape(b, n_q, SB_HEADS, SB_HEAD_DIM).transpose(0, 2, 1, 3)
    eye = jnp.eye(SB_HEADS, dtype=BF16)
    qbd = (q4[:, :, :, None, :] * eye[None, :, None, :, None]).reshape(b, SB_HEADS * n_q, SB_WIDTH)
    bias_col = jnp.repeat(bias.astype(F32), n_q).reshape(SB_HEADS * n_q, 1)
    pad_keys = lambda x: jnp.pad(x, ((0, 0), (0, page - n_q), (0, 0)))
    steps = n_pages // pages_per_step

    def page_map(u):
        return lambda s, g, pt: (pt[s, n_pages - 1 - (g * pages_per_step + u)], 0, 0, 0)

    page_specs = [pl.BlockSpec((1, SB_HEADS, SB_HEAD_DIM, page), page_map(u)) for u in range(pages_per_step)]
    per_seq = lambda shape: pl.BlockSpec((1,) + shape, lambda s, g, pt: (s, 0, 0))
    grid_spec = pltpu.PrefetchScalarGridSpec(
        num_scalar_prefetch=1,
        grid=(b, steps),
        in_specs=[per_seq((SB_HEADS * n_q, SB_WIDTH)),
                  pl.BlockSpec((SB_HEADS * n_q, 1), lambda s, g, pt: (0, 0)),
                  per_seq((page, SB_WIDTH)), per_seq((page, SB_WIDTH))] + page_specs + page_specs,
        out_specs=per_seq((n_q, SB_WIDTH)),
        scratch_shapes=[pltpu.VMEM((SB_HEADS * n_q, SB_WIDTH), F32),
                        pltpu.VMEM((SB_HEADS * n_q, page), F32)],
    )
    return pl.pallas_call(
        functools.partial(_sb_sample_kernel, pages_per_step=pages_per_step, n_q=n_q),
        grid_spec=grid_spec,
        out_shape=jax.ShapeDtypeStruct((b, n_q, SB_WIDTH), BF16),
        compiler_params=_params("parallel", "arbitrary"),
        name="sb_sample",
    )(page_table, qbd, bias_col, pad_keys(k_new), pad_keys(v_new),
      *([ck] * pages_per_step), *([cv] * pages_per_step))


def _gdn_kernel(hp_ref, qkv_ref, gate_ref, ab_ref, cbuf_ref, cw_ref, nw_ref, s0_ref,
                o_ref, s_out_ref, s_ref, carry_ref, *, chunk):
    c = pl.program_id(1)

    @pl.when(c == 0)
    def _():
        s_ref[...] = s0_ref[0]
        carry_ref[...] = cbuf_ref[0]

    x = qkv_ref[...]
    ext = jnp.concatenate([carry_ref[...], x], axis=0)
    conv = ext[5:5 + chunk] * cw_ref[0:1, :]
    for i in range(1, GDN_CONV):
        conv = conv + ext[5 + i:5 + i + chunk] * cw_ref[i:i + 1, :]
    carry_ref[...] = ext[chunk:chunk + 8]
    act = _silu(conv)

    rows = lax.broadcasted_iota(jnp.int32, (chunk, chunk), 0)
    cols = lax.broadcasted_iota(jnp.int32, (chunk, chunk), 1)
    lower = cols <= rows
    strict = cols < rows
    lower01 = jnp.where(lower, 1.0, 0.0)
    ab = ab_ref[...]
    d = GDN_HEAD_DIM
    heads = range(GDN_HEADS)
    head_cols = lambda x, base, h: x[:, base + h * d:base + (h + 1) * d]
    l2n = lambda x: x * lax.rsqrt(jnp.sum(x * x, axis=-1, keepdims=True) + NORM_EPS)
    qn = [l2n(head_cols(act, 0, h)) * (d ** -0.5) for h in heads]
    kn = [l2n(head_cols(act, GDN_WIDTH, h)) for h in heads]
    vh = [head_cols(act, 2 * GDN_WIDTH, h) for h in heads]
    beta = [_sigmoid(ab[:, GDN_HEADS + h:GDN_HEADS + h + 1]) for h in heads]
    g = [-jnp.exp(hp_ref[0:1, h:h + 1]) * _softplus(ab[:, h:h + 1] + hp_ref[1:2, h:h + 1]) for h in heads]
    gc = [_dot_exact_lhs(lower01, jnp.broadcast_to(g[h], (chunk, d))) for h in heads]
    gi = [gc[h][:, :chunk] if chunk <= d else jnp.broadcast_to(gc[h][:, :1], (chunk, chunk)) for h in heads]
    decay = [jnp.exp(jnp.where(lower, gi[h] - gi[h].T, -jnp.inf)) for h in heads]
    kb = [kn[h] * beta[h] for h in heads]
    tri = [jnp.where(strict, _dot_nt(kb[h], kn[h]) * decay[h], 0.0) for h in heads]
    attn = [jnp.where(lower, _dot_nt(qn[h], kn[h]) * decay[h], 0.0) for h in heads]
    t_inv = _inv_i_minus_many([-t for t in tri], chunk, chunk)
    e_gc = [jnp.exp(gc[h]) for h in heads]
    uw = [_dot(t_inv[h], jnp.concatenate([vh[h] * beta[h], kb[h] * e_gc[h]], axis=-1)) for h in heads]
    s = [s_ref[h] for h in heads]
    v_new = [uw[h][:, :d] - _dot_nt(uw[h][:, d:], s[h]) for h in heads]
    o = [_dot_nt(qn[h] * e_gc[h], s[h]) + _dot(attn[h], v_new[h]) for h in heads]
    for h in heads:
        g_last = gc[h][chunk - 1:chunk, :]
        s_ref[h] = s[h] * jnp.exp(g_last) + _dot_tn(v_new[h], kn[h] * jnp.exp(g_last - gc[h]))
    for h in heads:
        on = o[h] * lax.rsqrt(jnp.mean(o[h] * o[h], axis=-1, keepdims=True) + NORM_EPS) * nw_ref[...]
        o_ref[:, h * d:(h + 1) * d] = (on * _silu(gate_ref[:, h * d:(h + 1) * d])).astype(o_ref.dtype)

    @pl.when(c == pl.num_programs(1) - 1)
    def _():
        s_out_ref[0] = s_ref[...]


def _gdn(y_all, row0, n_seq, seq_len, chunk, conv_buf, state0, conv_w, a_log, dt_bias, norm_w):
    n_chunks = seq_len // chunk
    blk0 = row0 // chunk
    row_map = lambda col: (lambda s, c: (blk0 + s * n_chunks + c, col))
    cbuf = jnp.pad(conv_buf, ((0, 0), (8 - (GDN_CONV - 1), 0), (0, 0)))
    cw = jnp.pad(conv_w, ((0, 8 - GDN_CONV), (0, 0)))
    head_params = jnp.zeros((8, 128), F32).at[0, :GDN_HEADS].set(a_log).at[1, :GDN_HEADS].set(dt_bias)
    state_spec = pl.BlockSpec((1, GDN_HEADS, GDN_HEAD_DIM, GDN_HEAD_DIM), lambda s, c: (s, 0, 0, 0))
    o, s_new = pl.pallas_call(
        functools.partial(_gdn_kernel, chunk=chunk),
        grid=(n_seq, n_chunks),
        in_specs=[pl.BlockSpec((8, 128), lambda s, c: (0, 0)),
                  pl.BlockSpec((chunk, 3 * GDN_WIDTH), row_map(1)),
                  pl.BlockSpec((chunk, COL_BLOCK), row_map(6)),
                  pl.BlockSpec((chunk, COL_BLOCK), row_map(7)),
                  pl.BlockSpec((1, 8, 3 * GDN_WIDTH), lambda s, c: (s, 0, 0)),
                  pl.BlockSpec((8, 3 * GDN_WIDTH), lambda s, c: (0, 0)),
                  pl.BlockSpec((1, GDN_HEAD_DIM), lambda s, c: (0, 0)),
                  state_spec],
        out_specs=[pl.BlockSpec((chunk, GDN_WIDTH), lambda s, c: (s * n_chunks + c, 0)), state_spec],
        out_shape=[jax.ShapeDtypeStruct((n_seq * seq_len, GDN_WIDTH), BF16),
                   jax.ShapeDtypeStruct(state0.shape, F32)],
        scratch_shapes=[pltpu.VMEM((GDN_HEADS, GDN_HEAD_DIM, GDN_HEAD_DIM), F32),
                        pltpu.VMEM((8, 3 * GDN_WIDTH), F32)],
        compiler_params=_params("parallel", "arbitrary"),
        name="gdn",
    )(head_params, y_all, y_all, y_all, cbuf, cw, norm_w.reshape(1, GDN_HEAD_DIM),
      jnp.swapaxes(state0, -1, -2))
    return o, jnp.swapaxes(s_new, -1, -2)


def _rwkv_kernel(r_ref, k_ref, v_ref, lw_ref, a_ref, g_ref, kk_ref, ka_ref, rk_ref, lg_ref, lb_ref,
                 s0_ref, o_ref, s_out_ref, s_ref, *, chunk, group):
    c = pl.program_id(1)

    @pl.when(c == 0)
    def _():
        s_ref[...] = s0_ref[0]

    hd = RWKV_HEAD_DIM
    width = group * hd
    size = group * chunk
    rows = lax.broadcasted_iota(jnp.int32, (size, size), 0)
    cols = lax.broadcasted_iota(jnp.int32, (size, size), 1)
    same = rows // chunk == cols // chunk
    lower = jnp.logical_and(same, cols <= rows)
    strict = jnp.logical_and(same, cols < rows)
    r2 = lax.broadcasted_iota(jnp.int32, (size, width), 0)
    c2 = lax.broadcasted_iota(jnp.int32, (size, width), 1)
    own = r2 // chunk == c2 // hd
    groups = range(D_MODEL // width)
    cols_of = lambda gi: slice(gi * width, (gi + 1) * width)
    stack = lambda x, gi: jnp.where(own, jnp.concatenate([x[:, cols_of(gi)]] * group, axis=0), 0.0)

    tr = lax.broadcasted_iota(jnp.int32, (chunk, chunk), 0)
    tc = lax.broadcasted_iota(jnp.int32, (chunk, chunk), 1)
    lw_all = lw_ref[...]
    cw_all = _dot_exact_lhs(jnp.where(tc <= tr, 1.0, 0.0), lw_all)
    tot_all = jnp.sum(lw_all, axis=0, keepdims=True)

    r = [stack(r_ref[...], gi) for gi in groups]
    v = [stack(v_ref[...], gi) for gi in groups]
    a = [stack(a_ref[...], gi) for gi in groups]
    lw = [stack(lw_all, gi) for gi in groups]
    cw = [stack(cw_all, gi) for gi in groups]
    k_in = [stack(k_ref[...], gi) for gi in groups]
    kk = [k_in[gi] * kk_ref[:, cols_of(gi)] for gi in groups]
    kk = [x * lax.rsqrt(jnp.sum(x * x, axis=-1, keepdims=True) + NORM_EPS) for x in kk]
    k = [k_in[gi] * (1.0 + (a[gi] - 1.0) * ka_ref[:, cols_of(gi)]) for gi in groups]
    b = [kk[gi] * a[gi] for gi in groups]
    tot = [tot_all[:, cols_of(gi)] for gi in groups]
    e_neg = [jnp.exp(-cw[gi]) for gi in groups]
    e_end = [jnp.exp(jnp.where(own, tot[gi] - cw[gi], 0.0)) for gi in groups]
    at = [-kk[gi] * jnp.exp(cw[gi] - lw[gi]) for gi in groups]
    rt = [r[gi] * jnp.exp(cw[gi]) for gi in groups]
    bt = [b[gi] * e_neg[gi] for gi in groups]
    kt = [k[gi] * e_neg[gi] for gi in groups]
    a_ab = [jnp.where(strict, _dot_nt(at[gi], bt[gi]), 0.0) for gi in groups]
    a_ak = [jnp.where(strict, _dot_nt(at[gi], kt[gi]), 0.0) for gi in groups]
    a_rb = [jnp.where(lower, _dot_nt(rt[gi], bt[gi]), 0.0) for gi in groups]
    a_rk = [jnp.where(lower, _dot_nt(rt[gi], kt[gi]), 0.0) for gi in groups]
    t_inv = _inv_i_minus_many(a_ab, size, chunk)
    s = [s_ref[gi] for gi in groups]
    rhs = [_dot_nt(at[gi], s[gi]) + _dot(a_ak[gi], v[gi]) for gi in groups]
    o_past = [_dot_nt(rt[gi], s[gi]) + _dot(a_rk[gi], v[gi]) for gi in groups]
    u = [_dot(t_inv[gi], rhs[gi]) for gi in groups]
    o = [o_past[gi] + _dot(a_rb[gi], u[gi]) for gi in groups]
    for gi in groups:
        s_ref[gi] = (s[gi] * jnp.exp(tot[gi]) + _dot_tn(u[gi], b[gi] * e_end[gi])
                     + _dot_tn(v[gi], k[gi] * e_end[gi]))
    for gi in groups:
        sl = cols_of(gi)
        mu = jnp.sum(o[gi], axis=-1, keepdims=True) * (1.0 / hd)
        oc = jnp.where(own, o[gi] - mu, 0.0)
        var = jnp.sum(oc * oc, axis=-1, keepdims=True) * (1.0 / hd)
        on = oc * lax.rsqrt(var + RWKV_GN_EPS) * lg_ref[:, sl] + jnp.where(own, lb_ref[:, sl], 0.0)
        on = on + jnp.sum(r[gi] * k[gi] * rk_ref[:, sl], axis=-1, keepdims=True) * v[gi]
        out = on[0:chunk]
        for hh in range(1, group):
            out = out + on[hh * chunk:(hh + 1) * chunk]
        o_ref[:, sl] = (out * g_ref[:, sl]).astype(o_ref.dtype)

    @pl.when(c == pl.num_programs(1) - 1)
    def _():
        s_out_ref[0] = s_ref[...]


def _rwkv(r, k, v, lw, a, g, row0, n_seq, seq_len, chunk, state0, k_k, k_a, r_k, lnx_g, lnx_b, group=4):
    d = D_MODEL
    n_groups = RWKV_HEADS // group
    width = group * RWKV_HEAD_DIM
    n_chunks = seq_len // chunk
    blk0 = row0 // chunk
    tok = pl.BlockSpec((chunk, d), lambda s, c: (blk0 + s * n_chunks + c, 0))
    vec = pl.BlockSpec((1, d), lambda s, c: (0, 0))
    s5 = state0.reshape(n_seq, n_groups, group, RWKV_HEAD_DIM, RWKV_HEAD_DIM)
    eye = jnp.eye(group, dtype=F32)
    s_bd = (s5[:, :, :, :, None, :] * eye[None, None, :, None, :, None]).reshape(n_seq, n_groups, width, width)
    state_spec = pl.BlockSpec((1, n_groups, width, width), lambda s, c: (s, 0, 0, 0))
    o, s_new = pl.pallas_call(
        functools.partial(_rwkv_kernel, chunk=chunk, group=group),
        grid=(n_seq, n_chunks),
        in_specs=[tok] * 6 + [vec] * 5 + [state_spec],
        out_specs=[pl.BlockSpec((chunk, d), lambda s, c: (s * n_chunks + c, 0)), state_spec],
        out_shape=[jax.ShapeDtypeStruct((n_seq * seq_len, d), BF16),
                   jax.ShapeDtypeStruct(s_bd.shape, F32)],
        scratch_shapes=[pltpu.VMEM((n_groups, width, width), F32)],
        compiler_params=_params("parallel", "arbitrary"),
        name="rwkv_wkv",
    )(r, k, v, lw, a, g, k_k.reshape(1, d), k_a.reshape(1, d), r_k.reshape(1, d),
      lnx_g.reshape(1, d), lnx_b.reshape(1, d), s_bd)
    s6 = s_new.reshape(n_seq, n_groups, group, RWKV_HEAD_DIM, group, RWKV_HEAD_DIM)
    diag = jnp.stack([s6[:, :, h, :, h, :] for h in range(group)], axis=2)
    return o, diag.reshape(state0.shape)


def _forward(x_prompt, x_sample, cache_k, cache_v, page_table, state_gdn_conv, state_gdn,
             state_rwkv_shift, state_rwkv, p, *, tm, tm_proj, sb_tq, sb_tk, pages_per_step,
             chunk_prompt, gdn_chunk_sample, rwkv_chunk_sample, ffn_tf, tm_moe, moe_rows, moe_strip):
    bp, t_p, d = x_prompt.shape
    bs, t_s, _ = x_sample.shape
    assert bp == 1
    n_p, n_s = bp * t_p, bs * t_s
    x = jnp.concatenate([x_prompt.reshape(n_p, d), x_sample.reshape(n_s, d)], axis=0)

    w_in = p['w_in0']
    cut = 3 * SB_WIDTH + 3 * GDN_WIDTH
    w_pad = jnp.concatenate(
        [w_in[:, :cut], w_in[:, cut + 2 * GDN_HEADS:], w_in[:, cut:cut + 2 * GDN_HEADS],
         jnp.zeros((d, COL_BLOCK - 2 * GDN_HEADS), F32)], axis=1).astype(BF16)
    y = _matmul(x, w_pad, tm, COL_BLOCK)
    q = y[:, :SB_WIDTH]
    k_rows = y[:, SB_WIDTH:2 * SB_WIDTH]
    v_rows = y[:, 2 * SB_WIDTH:3 * SB_WIDTH]
    gdn_rows = y[:, 3 * SB_WIDTH:3 * SB_WIDTH + 3 * GDN_WIDTH]

    o_sb_p = _sb_prompt(q[:n_p], k_rows[:n_p], v_rows[:n_p], p['sb_bias'], sb_tq, sb_tk)
    shape_s = (bs, t_s, SB_WIDTH)
    o_sb_s = _sb_sample(q[n_p:].reshape(shape_s), k_rows[n_p:].reshape(shape_s), v_rows[n_p:].reshape(shape_s),
                        cache_k, cache_v, page_table, p['sb_bias'], pages_per_step)
    o_sb = jnp.concatenate([o_sb_p, o_sb_s.reshape(n_s, SB_WIDTH)], axis=0)

    gdn_args = (p['gdn_conv_w'], p['gdn_a_log'], p['gdn_dt_bias'], p['gdn_norm_w'])
    o_gdn_p, gdn_state_p = _gdn(y, 0, bp, t_p, chunk_prompt,
                                jnp.zeros((bp, GDN_CONV - 1, 3 * GDN_WIDTH), F32),
                                jnp.zeros((bp,) + state_gdn.shape[1:], F32), *gdn_args)
    o_gdn_s, gdn_state_s = _gdn(y, n_p, bs, t_s, gdn_chunk_sample, state_gdn_conv, state_gdn, *gdn_args)
    o_gdn = jnp.concatenate([o_gdn_p, o_gdn_s], axis=0)
    conv_p = gdn_rows[:n_p].reshape(bp, t_p, -1)[:, t_p - (GDN_CONV - 1):]
    conv_s = gdn_rows[n_p:].reshape(bs, t_s, -1)[:, t_s - (GDN_CONV - 1):]

    w_out = p['w_out0'].astype(BF16)
    x = _proj_ln([o_sb, o_gdn], [w_out[:SB_WIDTH], w_out[SB_WIDTH:]], x,
                 p['ln_gamma'][0, 0], p['ln_beta'][0, 0], tm, "out_proj_ln")
    x = _ffn_ln(x, p['ffn_gate'].astype(BF16), p['ffn_up'].astype(BF16), p['ffn_down'].astype(BF16),
                p['ln_gamma'][0, 1], p['ln_beta'][0, 1], tm, ffn_tf)

    x_p = x[:n_p].reshape(bp, t_p, d)
    x_s = x[n_p:].reshape(bs, t_s, d)
    prev_p = jnp.concatenate([jnp.zeros((bp, 1, d), F32), x_p[:, :-1]], axis=1)
    prev_s = jnp.concatenate([state_rwkv_shift[:, None, :], x_s[:, :-1]], axis=1)
    x_prev = jnp.concatenate([prev_p.reshape(n_p, d), prev_s.reshape(n_s, d)], axis=0)
    bf = lambda name: p[name].astype(BF16)
    r, k, v, lw, a, g = _rwkv_proj(
        x, x_prev, p['rwkv_mix'], bf('rwkv_w_r'), bf('rwkv_w_k'), bf('rwkv_w_v'), bf('rwkv_w1'), bf('rwkv_w2'),
        bf('rwkv_a1'), bf('rwkv_a2'), bf('rwkv_g1'), bf('rwkv_g2'),
        p['rwkv_w0'].reshape(1, d), p['rwkv_a0'].reshape(1, d), tm_proj)
    rwkv_args = (p['rwkv_k_k'], p['rwkv_k_a'], p['rwkv_r_k'], p['rwkv_lnx_g'], p['rwkv_lnx_b'])
    o_p, rwkv_state_p = _rwkv(r, k, v, lw, a, g, 0, bp, t_p, chunk_prompt,
                              jnp.zeros((bp,) + state_rwkv.shape[1:], F32), *rwkv_args)
    o_s, rwkv_state_s = _rwkv(r, k, v, lw, a, g, n_p, bs, t_s, rwkv_chunk_sample, state_rwkv, *rwkv_args)
    x1 = _proj_ln([jnp.concatenate([o_p, o_s], axis=0)], [bf('rwkv_w_o')], x,
                  p['ln_gamma'][1, 0], p['ln_beta'][1, 0], tm, "rwkv_out_ln")
    w_router = jnp.pad(p['moe_router'], ((0, 0), (0, 128 - N_EXPERTS)))
    out = _moe_ln(x1, bf('moe_gate'), bf('moe_up'), bf('moe_down'), p['ln_gamma'][1, 1], p['ln_beta'][1, 1],
                  w_router, tm_moe, ffn_tf, moe_rows, moe_strip)

    heads = lambda rows, b_, t_: rows.reshape(b_, t_, SB_HEADS, SB_HEAD_DIM)
    return (out[:n_p].reshape(bp, t_p, d), out[n_p:].reshape(bs, t_s, d),
            heads(k_rows[:n_p], bp, t_p), heads(v_rows[:n_p], bp, t_p), conv_p, gdn_state_p,
            x_p[:, -1], rwkv_state_p,
            heads(k_rows[n_p:], bs, t_s), heads(v_rows[n_p:], bs, t_s), conv_s, gdn_state_s,
            x_s[:, -1], rwkv_state_s)


def kernel(x_prompt, x_sample, cache_k, cache_v, page_table, state_gdn_conv, state_gdn, state_rwkv_shift, state_rwkv, w_in0, sb_bias, gdn_conv_w, gdn_a_log, gdn_dt_bias, gdn_norm_w, w_out0, ffn_gate, ffn_up, ffn_down, rwkv_mix, rwkv_w_r, rwkv_w_k, rwkv_w_v, rwkv_w0, rwkv_w1, rwkv_w2, rwkv_a0, rwkv_a1, rwkv_a2, rwkv_g1, rwkv_g2, rwkv_k_k, rwkv_k_a, rwkv_r_k, rwkv_lnx_g, rwkv_lnx_b, rwkv_w_o, moe_router, moe_gate, moe_up, moe_down, ln_gamma, ln_beta):
    p = dict(w_in0=w_in0, sb_bias=sb_bias, gdn_conv_w=gdn_conv_w, gdn_a_log=gdn_a_log, gdn_dt_bias=gdn_dt_bias,
             gdn_norm_w=gdn_norm_w, w_out0=w_out0, ffn_gate=ffn_gate, ffn_up=ffn_up, ffn_down=ffn_down,
             rwkv_mix=rwkv_mix, rwkv_w_r=rwkv_w_r, rwkv_w_k=rwkv_w_k, rwkv_w_v=rwkv_w_v,
             rwkv_w0=rwkv_w0, rwkv_w1=rwkv_w1, rwkv_w2=rwkv_w2, rwkv_a0=rwkv_a0, rwkv_a1=rwkv_a1,
             rwkv_a2=rwkv_a2, rwkv_g1=rwkv_g1, rwkv_g2=rwkv_g2, rwkv_k_k=rwkv_k_k, rwkv_k_a=rwkv_k_a,
             rwkv_r_k=rwkv_r_k, rwkv_lnx_g=rwkv_lnx_g, rwkv_lnx_b=rwkv_lnx_b, rwkv_w_o=rwkv_w_o,
             moe_router=moe_router, moe_gate=moe_gate, moe_up=moe_up, moe_down=moe_down,
             ln_gamma=ln_gamma, ln_beta=ln_beta)
    return _forward(x_prompt, x_sample, cache_k, cache_v, page_table, state_gdn_conv, state_gdn,
                    state_rwkv_shift, state_rwkv, p, tm=640, tm_proj=320, sb_tq=512, sb_tk=256,
                    pages_per_step=16, chunk_prompt=64, gdn_chunk_sample=8, rwkv_chunk_sample=8, ffn_tf=1408,
                    tm_moe=1280, moe_rows=384, moe_strip=256)
```

```python
import functools
import math

import jax
import jax.numpy as jnp
import numpy as np
from jax import lax
from jax.experimental import pallas as pl
from jax.experimental.pallas import tpu as pltpu

F32 = jnp.float32
BF16 = jnp.bfloat16

D_MODEL = 1024
SB_HEADS = 8
SB_HEAD_DIM = 64
SB_WIDTH = SB_HEADS * SB_HEAD_DIM
GDN_HEADS = 4
GDN_HEAD_DIM = 128
GDN_WIDTH = GDN_HEADS * GDN_HEAD_DIM
GDN_CONV = 4
RWKV_HEAD_DIM = 64
RWKV_HEADS = D_MODEL // RWKV_HEAD_DIM
RWKV_GN_EPS = 64e-5
N_EXPERTS = 8
DEPTH = 2
DEEPNORM_ALPHA = (2 * DEPTH) ** 0.25
LN_EPS = 1e-5
NORM_EPS = 1e-6
LOG2E = math.log2(math.e)

IN0_COLS = 4096
COL_BLOCK = 512
VMEM_LIMIT_BYTES = 56 * 1024 * 1024

_NT = (((1,), (1,)), ((), ()))
_TN = (((0,), (0,)), ((), ()))


def _params(*sem):
    return pltpu.CompilerParams(dimension_semantics=sem, vmem_limit_bytes=VMEM_LIMIT_BYTES)


def _dot(a, b):
    return jnp.dot(a.astype(BF16), b.astype(BF16), preferred_element_type=F32)


def _dot_nt(a, b):
    return lax.dot_general(a.astype(BF16), b.astype(BF16), _NT, preferred_element_type=F32)


def _dot_tn(a, b):
    return lax.dot_general(a.astype(BF16), b.astype(BF16), _TN, preferred_element_type=F32)


def _dot_exact_lhs(a01, x):
    a = a01.astype(BF16)
    x1 = x.astype(BF16)
    r1 = x - x1.astype(F32)
    x2 = r1.astype(BF16)
    x3 = (r1 - x2.astype(F32)).astype(BF16)
    out = jnp.dot(a, x1, preferred_element_type=F32)
    out = out + jnp.dot(a, x2, preferred_element_type=F32)
    return out + jnp.dot(a, x3, preferred_element_type=F32)


def _dot_exact_rhs(x, b01):
    b = b01.astype(BF16)
    x1 = x.astype(BF16)
    r1 = x - x1.astype(F32)
    x2 = r1.astype(BF16)
    x3 = (r1 - x2.astype(F32)).astype(BF16)
    out = jnp.dot(x1, b, preferred_element_type=F32)
    out = out + jnp.dot(x2, b, preferred_element_type=F32)
    return out + jnp.dot(x3, b, preferred_element_type=F32)


def _softplus(z):
    return jnp.maximum(z, 0.0) + jnp.log1p(jnp.exp(-jnp.abs(z)))


def _sigmoid(z):
    return 1.0 / (1.0 + jnp.exp(-z))


def _silu(z):
    return z * _sigmoid(z)


def _layer_norm(x, g, b):
    mu = jnp.mean(x, axis=-1, keepdims=True)
    xc = x - mu
    var = jnp.mean(xc * xc, axis=-1, keepdims=True)
    return xc * lax.rsqrt(var + LN_EPS) * g + b


def _inv_i_minus(n, size, block):
    return _inv_i_minus_many([n], size, block)[0]


def _inv_i_minus_many(ns, size, block):
    rows = lax.broadcasted_iota(jnp.int32, (size, size), 0)
    cols = lax.broadcasted_iota(jnp.int32, (size, size), 1)
    eye = jnp.where(rows == cols, 1.0, 0.0)
    ps = [eye + n for n in ns]
    ys = list(ns)
    for _ in range(max(0, int(math.ceil(math.log2(block))) - 1)):
        ys = [_dot(y, y) for y in ys]
        ps = [p + _dot(p, y) for p, y in zip(ps, ys)]
    return ps


def _matmul_kernel(x_ref, w_ref, o_ref, *, tn):
    xb = x_ref[...].astype(BF16)
    for j in range(w_ref.shape[1] // tn):
        o_ref[:, j * tn:(j + 1) * tn] = jnp.dot(xb, w_ref[:, j * tn:(j + 1) * tn], preferred_element_type=F32)


def _matmul(x, w, tm, tn):
    n, k = x.shape
    m = w.shape[1]
    return pl.pallas_call(
        functools.partial(_matmul_kernel, tn=tn),
        grid=(n // tm,),
        in_specs=[pl.BlockSpec((tm, k), lambda i: (i, 0)),
                  pl.BlockSpec((k, m), lambda i: (0, 0))],
        out_specs=pl.BlockSpec((tm, m), lambda i: (i, 0)),
        out_shape=jax.ShapeDtypeStruct((n, m), F32),
        compiler_params=_params("parallel"),
        name="in_proj",
    )(x, w)


def _proj_ln_kernel(*refs, n_in):
    a_refs = refs[:n_in]
    w_refs = refs[n_in:2 * n_in]
    x_ref, g_ref, b_ref, o_ref = refs[2 * n_in:]
    h = jnp.dot(a_refs[0][...], w_refs[0][...], preferred_element_type=F32)
    for a_ref, w_ref in zip(a_refs[1:], w_refs[1:]):
        h = h + jnp.dot(a_ref[...], w_ref[...], preferred_element_type=F32)
    o_ref[...] = _layer_norm(DEEPNORM_ALPHA * x_ref[...] + h, g_ref[...], b_ref[...])


def _proj_ln(acts, weights, x, gamma, beta, tm, name):
    n, d = x.shape
    n_in = len(acts)
    in_specs = [pl.BlockSpec((tm, a.shape[1]), lambda i: (i, 0)) for a in acts]
    in_specs += [pl.BlockSpec(w.shape, lambda i: (0, 0)) for w in weights]
    in_specs += [pl.BlockSpec((tm, d), lambda i: (i, 0)),
                 pl.BlockSpec((1, d), lambda i: (0, 0)),
                 pl.BlockSpec((1, d), lambda i: (0, 0))]
    return pl.pallas_call(
        functools.partial(_proj_ln_kernel, n_in=n_in),
        grid=(n // tm,),
        in_specs=in_specs,
        out_specs=pl.BlockSpec((tm, d), lambda i: (i, 0)),
        out_shape=jax.ShapeDtypeStruct((n, d), F32),
        compiler_params=_params("parallel"),
        name=name,
    )(*acts, *weights, x, gamma.reshape(1, d), beta.reshape(1, d))


def _top2_gates(logits):
    lane = lax.broadcasted_iota(jnp.int32, logits.shape, 1).astype(F32)
    big = float(logits.shape[1])
    lg = jnp.where(lane < N_EXPERTS, logits, -jnp.inf)
    m1 = jnp.max(lg, axis=-1, keepdims=True)
    i1 = jnp.min(jnp.where(lg == m1, lane, big), axis=-1, keepdims=True)
    lg2 = jnp.where(lane == i1, -jnp.inf, lg)
    m2 = jnp.max(lg2, axis=-1, keepdims=True)
    i2 = jnp.min(jnp.where(lg2 == m2, lane, big), axis=-1, keepdims=True)
    e2 = jnp.exp(m2 - m1)
    den = 1.0 + e2
    gates = jnp.where(lane == i1, 1.0 / den, 0.0) + jnp.where(lane == i2, e2 / den, 0.0)
    chosen = jnp.where(jnp.logical_or(lane == i1, lane == i2), 1.0, 0.0)
    return gates, chosen


def _ffn_ln_kernel(x_ref, wg_ref, wu_ref, wd_ref, g_ref, b_ref, o_ref, acc_ref):
    f = pl.program_id(1)

    @pl.when(f == 0)
    def _():
        acc_ref[...] = jnp.zeros_like(acc_ref)

    xb = x_ref[...].astype(BF16)
    hg = jnp.dot(xb, wg_ref[...], preferred_element_type=F32)
    hu = jnp.dot(xb, wu_ref[...], preferred_element_type=F32)
    acc_ref[...] += jnp.dot((_silu(hg) * hu).astype(BF16), wd_ref[...], preferred_element_type=F32)

    @pl.when(f == pl.num_programs(1) - 1)
    def _():
        o_ref[...] = _layer_norm(DEEPNORM_ALPHA * x_ref[...] + acc_ref[...], g_ref[...], b_ref[...])


def _ffn_ln(x, w_gate, w_up, w_down, gamma, beta, tm, tf):
    n, d = x.shape
    d_ff = w_gate.shape[1]
    return pl.pallas_call(
        _ffn_ln_kernel,
        grid=(n // tm, d_ff // tf),
        in_specs=[pl.BlockSpec((tm, d), lambda i, f: (i, 0)),
                  pl.BlockSpec((d, tf), lambda i, f: (0, f)),
                  pl.BlockSpec((d, tf), lambda i, f: (0, f)),
                  pl.BlockSpec((tf, d), lambda i, f: (f, 0)),
                  pl.BlockSpec((1, d), lambda i, f: (0, 0)),
                  pl.BlockSpec((1, d), lambda i, f: (0, 0))],
        out_specs=pl.BlockSpec((tm, d), lambda i, f: (i, 0)),
        out_shape=jax.ShapeDtypeStruct((n, d), F32),
        scratch_shapes=[pltpu.VMEM((tm, d), F32)],
        compiler_params=_params("parallel", "arbitrary"),
        name="ffn_ln",
    )(x, w_gate, w_up, w_down, gamma.reshape(1, d), beta.reshape(1, d))


def _moe_kernel(x_ref, wg_ref, wu_ref, wd_ref, g_ref, b_ref, wr_ref, o_ref,
                xb_ref, gate_ref, key_ref, keyt_ref, xe_ref, ye_ref, *, rows, strip):
    tm, d = x_ref.shape
    e = pl.program_id(1)
    f = pl.program_id(2)
    n_f = pl.num_programs(2)
    n_lane = gate_ref.shape[1]

    @pl.when(jnp.logical_and(e == 0, f == 0))
    def _():
        x = x_ref[...]
        xb_ref[...] = x.astype(BF16)
        logits = jnp.dot(x, wr_ref[...], preferred_element_type=F32, precision=lax.Precision.HIGHEST)
        gates, chosen = _top2_gates(logits)
        gate_ref[...] = gates
        tr = lax.broadcasted_iota(jnp.int32, (tm, tm), 0)
        tc = lax.broadcasted_iota(jnp.int32, (tm, tm), 1)
        before = jnp.where(tc < tr, 1.0, 0.0).astype(BF16)
        rank = jnp.dot(before, chosen.astype(BF16), preferred_element_type=F32)
        key = jnp.where(chosen > 0.0, rank, -1.0)
        key_ref[...] = key
        keyt_ref[...] = key.T
        o_ref[...] = jnp.zeros_like(o_ref)

    lane = lax.broadcasted_iota(jnp.int32, (tm, n_lane), 1)
    key_col = jnp.sum(jnp.where(lane == e, key_ref[...], 0.0), axis=-1, keepdims=True)
    count = jnp.sum(jnp.where(key_col >= 0.0, 1.0, 0.0)).astype(jnp.int32)
    n_chunks = (count + rows - 1) // rows

    @pl.when(f == 0)
    def _():
        key_row = keyt_ref[pl.ds(e, 1), :]

        def gather(c, carry):
            slot = (c * rows + lax.broadcasted_iota(jnp.int32, (rows, tm), 0)).astype(F32)
            sel = jnp.where(key_row == slot, 1.0, 0.0).astype(BF16)
            start = pl.multiple_of(c * rows, rows)
            xe_ref[pl.ds(start, rows), :] = jnp.dot(sel, xb_ref[...], preferred_element_type=F32).astype(BF16)
            return carry

        lax.fori_loop(0, n_chunks, gather, 0)

    def expert(c, carry):
        start = pl.multiple_of(c * rows, rows)
        xe = xe_ref[pl.ds(start, rows), :]
        hg = jnp.dot(xe, wg_ref[0], preferred_element_type=F32)
        hu = jnp.dot(xe, wu_ref[0], preferred_element_type=F32)
        y = jnp.dot((_silu(hg) * hu).astype(BF16), wd_ref[0], preferred_element_type=F32)

        @pl.when(f == 0)
        def _():
            ye_ref[pl.ds(start, rows), :] = y

        @pl.when(f > 0)
        def _():
            ye_ref[pl.ds(start, rows), :] += y

        return carry

    lax.fori_loop(0, n_chunks, expert, 0)

    @pl.when(f == n_f - 1)
    def _():
        gate_col = jnp.sum(jnp.where(lane == e, gate_ref[...], 0.0), axis=-1, keepdims=True)

        def scatter(c, carry):
            start = pl.multiple_of(c * rows, rows)
            y = ye_ref[pl.ds(start, rows), :].astype(BF16)
            for s in range(tm // strip):
                tok = slice(s * strip, (s + 1) * strip)
                slot = (c * rows + lax.broadcasted_iota(jnp.int32, (strip, rows), 1)).astype(F32)
                sel = jnp.where(key_col[tok] == slot, 1.0, 0.0).astype(BF16)
                o_ref[tok, :] += gate_col[tok] * jnp.dot(sel, y, preferred_element_type=F32)
            return carry

        lax.fori_loop(0, n_chunks, scatter, 0)

    @pl.when(jnp.logical_and(e == pl.num_programs(1) - 1, f == n_f - 1))
    def _():
        for s in range(tm // strip):
            tok = slice(s * strip, (s + 1) * strip)
            o_ref[tok, :] = _layer_norm(DEEPNORM_ALPHA * x_ref[tok, :] + o_ref[tok, :], g_ref[...], b_ref[...])


def _moe_ln(x, w_gate, w_up, w_down, gamma, beta, w_router, tm, tf, rows, strip):
    n, d = x.shape
    n_e, _, d_ff = w_gate.shape
    assert tm % strip == 0
    cap = -(-tm // rows) * rows
    once = dict(pipeline_mode=pl.Buffered(1))
    return pl.pallas_call(
        functools.partial(_moe_kernel, rows=rows, strip=strip),
        grid=(n // tm, n_e, d_ff // tf),
        in_specs=[pl.BlockSpec((tm, d), lambda i, e, f: (i, 0), **once),
                  pl.BlockSpec((1, d, tf), lambda i, e, f: (e, 0, f)),
                  pl.BlockSpec((1, d, tf), lambda i, e, f: (e, 0, f)),
                  pl.BlockSpec((1, tf, d), lambda i, e, f: (e, f, 0)),
                  pl.BlockSpec((1, d), lambda i, e, f: (0, 0)),
                  pl.BlockSpec((1, d), lambda i, e, f: (0, 0)),
                  pl.BlockSpec(w_router.shape, lambda i, e, f: (0, 0))],
        out_specs=pl.BlockSpec((tm, d), lambda i, e, f: (i, 0)),
        out_shape=jax.ShapeDtypeStruct((n, d), F32),
        scratch_shapes=[pltpu.VMEM((tm, d), BF16),
                        pltpu.VMEM((tm, w_router.shape[1]), F32),
                        pltpu.VMEM((tm, w_router.shape[1]), F32),
                        pltpu.VMEM((w_router.shape[1], tm), F32),
                        pltpu.VMEM((cap, d), BF16),
                        pltpu.VMEM((cap, d), F32)],
        compiler_params=_params("parallel", "arbitrary", "arbitrary"),
        name="moe_ln",
    )(x, w_gate, w_up, w_down, gamma.reshape(1, d), beta.reshape(1, d), w_router)


def _rwkv_proj_kernel(x_ref, xp_ref, mix_ref, wr_ref, wk_ref, wv_ref, w1_ref, w2_ref, a1_ref, a2_ref,
                      g1_ref, g2_ref, w0_ref, a0_ref, r_ref, k_ref, v_ref, lw_ref, a_ref, g_ref):
    x = x_ref[...]
    xx = xp_ref[...] - x
    mixed = lambda i: (x + xx * mix_ref[i:i + 1, :]).astype(BF16)
    r_ref[...] = jnp.dot(mixed(0), wr_ref[...], preferred_element_type=F32)
    w_lora = _dot(jnp.tanh(jnp.dot(mixed(1), w1_ref[...], preferred_element_type=F32)), w2_ref[...])
    lw_ref[...] = -jnp.exp(-_softplus(-(w0_ref[...] + w_lora)) - 0.5)
    k_ref[...] = jnp.dot(mixed(2), wk_ref[...], preferred_element_type=F32)
    v_ref[...] = jnp.dot(mixed(3), wv_ref[...], preferred_element_type=F32)
    a_lora = _dot(jnp.dot(mixed(4), a1_ref[...], preferred_element_type=F32), a2_ref[...])
    a_ref[...] = _sigmoid(a0_ref[...] + a_lora)
    g_ref[...] = _dot(_sigmoid(jnp.dot(mixed(5), g1_ref[...], preferred_element_type=F32)), g2_ref[...])


def _rwkv_proj(x, x_prev, mix, w_r, w_k, w_v, w1, w2, a1, a2, g1, g2, w0, a0, tm):
    n, d = x.shape
    tok = pl.BlockSpec((tm, d), lambda i: (i, 0))
    full = lambda a: pl.BlockSpec(a.shape, lambda i: (0, 0))
    consts = [mix, w_r, w_k, w_v, w1, w2, a1, a2, g1, g2, w0, a0]
    return pl.pallas_call(
        _rwkv_proj_kernel,
        grid=(n // tm,),
        in_specs=[tok, tok] + [full(c) for c in consts],
        out_specs=[tok] * 6,
        out_shape=[jax.ShapeDtypeStruct((n, d), F32)] * 6,
        compiler_params=_params("parallel"),
        name="rwkv_proj",
    )(x, x_prev, *consts)


def _suffix_sum_rows8(x):
    row = lax.broadcasted_iota(jnp.int32, x.shape, 0)
    for sh in (1, 2, 4):
        shifted = pltpu.roll(x, 8 - sh, axis=0)
        x = x + jnp.where(row < 8 - sh, shifted, 0.0)
    return x


def _sb_block(z_ref, r_ref, a_ref, carry, visible):
    tk, tq = z_ref.shape
    nsub = tk // 8
    run = jnp.zeros((8, tq), F32)
    for s in reversed(range(nsub)):
        rows = slice(s * 8, (s + 1) * 8)
        z = z_ref[rows, :]
        neg_abs = pltpu.bitcast(pltpu.bitcast(z, jnp.uint32) | jnp.uint32(0x80000000), F32)
        sp = jnp.maximum(z, 0.0) + jnp.log2(1.0 + jnp.exp2(neg_abs))
        vis = visible(s * 8, 8)
        if vis is None:
            run = run + sp
            r_ref[rows, :] = z - run
        else:
            r_ref[rows, :] = (z - sp) - run
            run = run + jnp.where(vis, sp, 0.0)
    incl = _suffix_sum_rows8(run)
    offset = incl - run + carry
    offset2 = jnp.concatenate([offset, offset], axis=0)
    for s in range(nsub // 2):
        rows = slice(s * 16, (s + 1) * 16)
        a = jnp.exp2(r_ref[rows, :] - offset2)
        vis = visible(s * 16, 16)
        if vis is not None:
            a = jnp.where(vis, a, 0.0)
        a_ref[rows, :] = a.astype(BF16)
    return carry + incl[0:1]


def _sb_prompt_kernel(q_ref, k_ref, vt_ref, o_ref, z0_ref, z1_ref, a0_ref, a1_ref, *, tq, tk):
    assert tq == 2 * tk
    i = pl.program_id(1)
    q = q_ref[0]
    nsub = tk // 8

    def scores(j):
        start = pl.multiple_of(jnp.maximum(j, 0) * tk, tk)
        return lax.dot_general(k_ref[0, pl.ds(start, tk), :], q, _NT, preferred_element_type=F32)

    def weighted_v(j, a_ref):
        start = pl.multiple_of(j * tk, tk)
        return jnp.dot(vt_ref[0, :, pl.ds(start, tk)], a_ref[...], preferred_element_type=F32)

    def weights(z_ref, a_ref, j, carry, masked):
        def visible(first_row, n_rows):
            if not masked:
                return None
            row = first_row + lax.broadcasted_iota(jnp.int32, (n_rows, tq), 0)
            col = lax.broadcasted_iota(jnp.int32, (n_rows, tq), 1)
            k_pos = j * tk + (row % 8) * nsub + row // 8
            return k_pos < i * tq + col
        return _sb_block(z_ref, z_ref, a_ref, carry, visible)

    newest = 2 * i + 1
    acc = jnp.zeros((SB_HEAD_DIM, tq), F32)
    carry = jnp.zeros((1, tq), F32)
    z0_ref[...] = scores(newest)
    z1_ref[...] = scores(newest - 1)
    carry = weights(z0_ref, a0_ref, newest, carry, True)
    z0_ref[...] = scores(newest - 2)
    acc = acc + weighted_v(newest, a0_ref)
    carry = weights(z1_ref, a1_ref, newest - 1, carry, True)

    def body(t, state):
        acc, carry = state
        blk = newest - 2 - 2 * t
        z1_ref[...] = scores(blk - 1)
        acc = acc + weighted_v(blk + 1, a1_ref)
        carry = weights(z0_ref, a0_ref, blk, carry, False)
        z0_ref[...] = scores(blk - 2)
        acc = acc + weighted_v(blk, a0_ref)
        carry = weights(z1_ref, a1_ref, blk - 1, carry, False)
        return acc, carry

    acc, carry = lax.fori_loop(0, i, body, (acc, carry))
    acc = acc + weighted_v(0, a1_ref)
    o_ref[0] = acc.astype(o_ref.dtype)


def _bf16_pieces(x, n):
    out = []
    for _ in range(n):
        piece = x.astype(BF16)
        out.append(piece)
        x = x - piece.astype(F32)
    return out


def _sb_prompt(q, k, v, bias, tq, tk):
    t = q.shape[0]
    nblk, nsub = t // tk, tk // 8
    pad = 128 - SB_HEAD_DIM
    qh = (q * (SB_HEAD_DIM ** -0.5 * LOG2E)).astype(BF16).reshape(t, SB_HEADS, SB_HEAD_DIM).transpose(1, 0, 2)
    bias_cols = jnp.stack(_bf16_pieces(bias.astype(F32) * LOG2E, 3), axis=-1)
    q_extra = jnp.pad(bias_cols, ((0, 0), (0, pad - 3)))[:, None, :]
    q_aug = jnp.concatenate([qh, jnp.broadcast_to(q_extra, (SB_HEADS, t, pad))], axis=-1)
    k5 = k.astype(BF16).reshape(nblk, 8, nsub, SB_HEADS, SB_HEAD_DIM)
    kh = k5.transpose(3, 0, 2, 1, 4).reshape(SB_HEADS, t, SB_HEAD_DIM)
    k_extra = jnp.pad(jnp.ones((3,), BF16), (0, pad - 3))
    k_aug = jnp.concatenate([kh, jnp.broadcast_to(k_extra, (SB_HEADS, t, pad))], axis=-1)
    v5 = v.astype(BF16).reshape(nblk, 8, nsub, SB_HEADS, SB_HEAD_DIM)
    vt = v5.transpose(3, 4, 0, 2, 1).reshape(SB_HEADS, SB_HEAD_DIM, t)
    out_t = pl.pallas_call(
        functools.partial(_sb_prompt_kernel, tq=tq, tk=tk),
        grid=(SB_HEADS, t // tq),
        in_specs=[pl.BlockSpec((1, tq, 128), lambda h, i: (h, i, 0)),
                  pl.BlockSpec((1, t, 128), lambda h, i: (h, 0, 0)),
                  pl.BlockSpec((1, SB_HEAD_DIM, t), lambda h, i: (h, 0, 0))],
        out_specs=pl.BlockSpec((1, SB_HEAD_DIM, tq), lambda h, i: (h, 0, i)),
        out_shape=jax.ShapeDtypeStruct((SB_HEADS, SB_HEAD_DIM, t), BF16),
        scratch_shapes=[pltpu.VMEM((tk, tq), F32), pltpu.VMEM((tk, tq), F32),
                        pltpu.VMEM((tk, tq), BF16), pltpu.VMEM((tk, tq), BF16)],
        compiler_params=_params("parallel", "parallel"),
        name="sb_prompt",
    )(q_aug, k_aug, vt)
    return out_t.transpose(2, 0, 1).reshape(t, SB_WIDTH)


def _sb_sample_kernel(pt_ref, qbd_ref, bias_ref, kn_ref, vn_ref, *rest, pages_per_step, n_q):
    del pt_ref
    k_refs = rest[:pages_per_step]
    v_refs = rest[pages_per_step:2 * pages_per_step]
    o_ref, acc_ref, carry_ref = rest[2 * pages_per_step:]
    g = pl.program_id(1)
    qbd = qbd_ref[0]
    bias = bias_ref[...]
    page = k_refs[0].shape[3]
    kj = lax.broadcasted_iota(jnp.int32, (page, 2 * page), 0)
    ks = lax.broadcasted_iota(jnp.int32, (page, 2 * page), 1)
    later_or_all = jnp.where(jnp.logical_or(ks >= page, kj > ks), 1.0, 0.0)

    @pl.when(g == 0)
    def _():
        z = _dot_nt(qbd, kn_ref[0]) + bias
        t = lax.broadcasted_iota(jnp.int32, z.shape, 0) % n_q
        s = lax.broadcasted_iota(jnp.int32, z.shape, 1)
        vis = s < t
        sp = _softplus(z)
        sums = _dot_exact_rhs(jnp.where(vis, sp, 0.0), later_or_all)
        a = jnp.where(vis, jnp.exp(z - sp - sums[:, :page]), 0.0)
        acc_ref[...] = _dot(a, vn_ref[0])
        carry_ref[...] = sums[:, page:]

    acc = acc_ref[...]
    carry = carry_ref[...]
    flat = lambda ref: ref[0].reshape(SB_WIDTH, page)
    zs = [_dot(qbd, flat(k_ref)) + bias for k_ref in k_refs]
    sps = [_softplus(z) for z in zs]
    sums = [_dot_exact_rhs(sp, later_or_all) for sp in sps]
    for z, sp, sm, v_ref in zip(zs, sps, sums, v_refs):
        a = jnp.exp(z - sp - sm[:, :page] - carry)
        acc = acc + _dot_nt(a, flat(v_ref))
        carry = carry + sm[:, page:]
    acc_ref[...] = acc
    carry_ref[...] = carry

    @pl.when(g == pl.num_programs(1) - 1)
    def _():
        r = lax.broadcasted_iota(jnp.int32, acc.shape, 0)
        c = lax.broadcasted_iota(jnp.int32, acc.shape, 1)
        own = jnp.where(r // n_q == c // SB_HEAD_DIM, acc, 0.0)
        o_ref[0] = jnp.sum(own.reshape(SB_HEADS, n_q, SB_WIDTH), axis=0).astype(o_ref.dtype)


def _sb_sample(q, k_new, v_new, cache_k, cache_v, page_table, bias, pages_per_step):
    b, n_q, _ = q.shape
    n_pages = page_table.shape[1]
    page = cache_k.shape[1]
    ck = jnp.transpose(cache_k, (0, 2, 3, 1))
    cv = jnp.transpose(cache_v, (0, 2, 3, 1))
    q4 = (q * SB_HEAD_DIM ** -0.5).astype(BF16).reshape(b, n_q, SB_HEADS, SB_HEAD_DIM).transpose(0, 2, 1, 3)
    eye = jnp.eye(SB_HEADS, dtype=BF16)
    qbd = (q4[:, :, :, None, :] * eye[None, :, None, :, None]).reshape(b, SB_HEADS * n_q, SB_WIDTH)
    bias_col = jnp.repeat(bias.astype(F32), n_q).reshape(SB_HEADS * n_q, 1)
    pad_keys = lambda x: jnp.pad(x, ((0, 0), (0, page - n_q), (0, 0)))
    steps = n_pages // pages_per_step

    def page_map(u):
        return lambda s, g, pt: (pt[s, n_pages - 1 - (g * pages_per_step + u)], 0, 0, 0)

    page_specs = [pl.BlockSpec((1, SB_HEADS, SB_HEAD_DIM, page), page_map(u)) for u in range(pages_per_step)]
    per_seq = lambda shape: pl.BlockSpec((1,) + shape, lambda s, g, pt: (s, 0, 0))
    grid_spec = pltpu.PrefetchScalarGridSpec(
        num_scalar_prefetch=1,
        grid=(b, steps),
        in_specs=[per_seq((SB_HEADS * n_q, SB_WIDTH)),
                  pl.BlockSpec((SB_HEADS * n_q, 1), lambda s, g, pt: (0, 0)),
                  per_seq((page, SB_WIDTH)), per_seq((page, SB_WIDTH))] + page_specs + page_specs,
        out_specs=per_seq((n_q, SB_WIDTH)),
        scratch_shapes=[pltpu.VMEM((SB_HEADS * n_q, SB_WIDTH), F32),
                        pltpu.VMEM((SB_HEADS * n_q, page), F32)],
    )
    return pl.pallas_call(
        functools.partial(_sb_sample_kernel, pages_per_step=pages_per_step, n_q=n_q),
        grid_spec=grid_spec,
        out_shape=jax.ShapeDtypeStruct((b, n_q, SB_WIDTH), BF16),
        compiler_params=_params("parallel", "arbitrary"),
        name="sb_sample",
    )(page_table, qbd, bias_col, pad_keys(k_new), pad_keys(v_new),
      *([ck] * pages_per_step), *([cv] * pages_per_step))


def _gdn_kernel(hp_ref, qkv_ref, gate_ref, ab_ref, cbuf_ref, cw_ref, nw_ref, s0_ref,
                o_ref, s_out_ref, s_ref, carry_ref, *, chunk, n_sub):
    c = pl.program_id(1)

    @pl.when(c == 0)
    def _():
        s_ref[...] = s0_ref[0]
        carry_ref[...] = cbuf_ref[0]

    n_rows = chunk * n_sub
    x = qkv_ref[...]
    ext = jnp.concatenate([carry_ref[...], x], axis=0)
    conv = ext[5:5 + n_rows] * cw_ref[0:1, :]
    for i in range(1, GDN_CONV):
        conv = conv + ext[5 + i:5 + i + n_rows] * cw_ref[i:i + 1, :]
    carry_ref[...] = ext[n_rows:n_rows + 8]
    act = _silu(conv)

    rows = lax.broadcasted_iota(jnp.int32, (chunk, chunk), 0)
    cols = lax.broadcasted_iota(jnp.int32, (chunk, chunk), 1)
    lower = cols <= rows
    strict = cols < rows
    lower01 = jnp.where(lower, 1.0, 0.0)
    ab = ab_ref[...]
    d = GDN_HEAD_DIM
    heads = range(GDN_HEADS)
    units = [(j, h) for j in range(n_sub) for h in heads]
    unit = lambda j, h: j * GDN_HEADS + h
    rows_of = lambda j: slice(j * chunk, (j + 1) * chunk)
    head_cols = lambda x, base, j, h: x[rows_of(j), base + h * d:base + (h + 1) * d]
    l2n = lambda x: x * lax.rsqrt(jnp.sum(x * x, axis=-1, keepdims=True) + NORM_EPS)
    qn = [l2n(head_cols(act, 0, j, h)) * (d ** -0.5) for j, h in units]
    kn = [l2n(head_cols(act, GDN_WIDTH, j, h)) for j, h in units]
    vh = [head_cols(act, 2 * GDN_WIDTH, j, h) for j, h in units]
    beta = [_sigmoid(ab[rows_of(j), GDN_HEADS + h:GDN_HEADS + h + 1]) for j, h in units]
    g = [-jnp.exp(hp_ref[0:1, h:h + 1]) * _softplus(ab[rows_of(j), h:h + 1] + hp_ref[1:2, h:h + 1])
         for j, h in units]
    gc = [_dot_exact_lhs(lower01, jnp.broadcast_to(x, (chunk, d))) for x in g]
    gi = [x[:, :chunk] if chunk <= d else jnp.broadcast_to(x[:, :1], (chunk, chunk)) for x in gc]
    decay = [jnp.exp(jnp.where(lower, x - x.T, -jnp.inf)) for x in gi]
    kb = [k_ * b_ for k_, b_ in zip(kn, beta)]
    tri = [jnp.where(strict, _dot_nt(kb_, k_) * dc, 0.0) for kb_, k_, dc in zip(kb, kn, decay)]
    attn = [jnp.where(lower, _dot_nt(q_, k_) * dc, 0.0) for q_, k_, dc in zip(qn, kn, decay)]
    t_inv = _inv_i_minus_many([-t for t in tri], chunk, chunk)
    e_gc = [jnp.exp(x) for x in gc]
    uw = [_dot(t_inv[i], jnp.concatenate([vh[i] * beta[i], kb[i] * e_gc[i]], axis=-1)) for i in range(len(units))]
    q_in = [q_ * e_ for q_, e_ in zip(qn, e_gc)]
    g_last = [x[chunk - 1:chunk, :] for x in gc]
    k_out = [k_ * jnp.exp(gl - x) for k_, gl, x in zip(kn, g_last, gc)]
    s = [s_ref[h] for h in heads]
    o = [None] * len(units)
    for j in range(n_sub):
        ids = [unit(j, h) for h in heads]
        v_new = [uw[i][:, :d] - _dot_nt(uw[i][:, d:], s[h]) for h, i in zip(heads, ids)]
        for h, i in zip(heads, ids):
            o[i] = _dot_nt(q_in[i], s[h]) + _dot(attn[i], v_new[h])
        s = [s[h] * jnp.exp(g_last[i]) + _dot_tn(v_new[h], k_out[i]) for h, i in zip(heads, ids)]
    for h in heads:
        s_ref[h] = s[h]
    for j, h in units:
        oi = o[unit(j, h)]
        on = oi * lax.rsqrt(jnp.mean(oi * oi, axis=-1, keepdims=True) + NORM_EPS) * nw_ref[...]
        gate = gate_ref[rows_of(j), h * d:(h + 1) * d]
        o_ref[rows_of(j), h * d:(h + 1) * d] = (on * _silu(gate)).astype(o_ref.dtype)

    @pl.when(c == pl.num_programs(1) - 1)
    def _():
        s_out_ref[0] = s_ref[...]


def _gdn(y_all, row0, n_seq, seq_len, chunk, n_sub, conv_buf, state0, conv_w, a_log, dt_bias, norm_w):
    step_rows = chunk * n_sub
    n_chunks = seq_len // step_rows
    blk0 = row0 // step_rows
    row_map = lambda col: (lambda s, c: (blk0 + s * n_chunks + c, col))
    cbuf = jnp.pad(conv_buf, ((0, 0), (8 - (GDN_CONV - 1), 0), (0, 0)))
    cw = jnp.pad(conv_w, ((0, 8 - GDN_CONV), (0, 0)))
    head_params = jnp.zeros((8, 128), F32).at[0, :GDN_HEADS].set(a_log).at[1, :GDN_HEADS].set(dt_bias)
    state_spec = pl.BlockSpec((1, GDN_HEADS, GDN_HEAD_DIM, GDN_HEAD_DIM), lambda s, c: (s, 0, 0, 0))
    o, s_new = pl.pallas_call(
        functools.partial(_gdn_kernel, chunk=chunk, n_sub=n_sub),
        grid=(n_seq, n_chunks),
        in_specs=[pl.BlockSpec((8, 128), lambda s, c: (0, 0)),
                  pl.BlockSpec((step_rows, 3 * GDN_WIDTH), row_map(1)),
                  pl.BlockSpec((step_rows, COL_BLOCK), row_map(6)),
                  pl.BlockSpec((step_rows, COL_BLOCK), row_map(7)),
                  pl.BlockSpec((1, 8, 3 * GDN_WIDTH), lambda s, c: (s, 0, 0)),
                  pl.BlockSpec((8, 3 * GDN_WIDTH), lambda s, c: (0, 0)),
                  pl.BlockSpec((1, GDN_HEAD_DIM), lambda s, c: (0, 0)),
                  state_spec],
        out_specs=[pl.BlockSpec((step_rows, GDN_WIDTH), lambda s, c: (s * n_chunks + c, 0)), state_spec],
        out_shape=[jax.ShapeDtypeStruct((n_seq * seq_len, GDN_WIDTH), BF16),
                   jax.ShapeDtypeStruct(state0.shape, F32)],
        scratch_shapes=[pltpu.VMEM((GDN_HEADS, GDN_HEAD_DIM, GDN_HEAD_DIM), F32),
                        pltpu.VMEM((8, 3 * GDN_WIDTH), F32)],
        compiler_params=_params("parallel", "arbitrary"),
        name="gdn",
    )(head_params, y_all, y_all, y_all, cbuf, cw, norm_w.reshape(1, GDN_HEAD_DIM),
      jnp.swapaxes(state0, -1, -2))
    return o, jnp.swapaxes(s_new, -1, -2)


def _rwkv_kernel(r_ref, k_ref, v_ref, lw_ref, a_ref, g_ref, kk_ref, ka_ref, rk_ref, lg_ref, lb_ref,
                 s0_ref, o_ref, s_out_ref, s_ref, *, chunk, group):
    c = pl.program_id(1)

    @pl.when(c == 0)
    def _():
        s_ref[...] = s0_ref[0]

    hd = RWKV_HEAD_DIM
    width = group * hd
    size = group * chunk
    rows = lax.broadcasted_iota(jnp.int32, (size, size), 0)
    cols = lax.broadcasted_iota(jnp.int32, (size, size), 1)
    same = rows // chunk == cols // chunk
    lower = jnp.logical_and(same, cols <= rows)
    strict = jnp.logical_and(same, cols < rows)
    r2 = lax.broadcasted_iota(jnp.int32, (size, width), 0)
    c2 = lax.broadcasted_iota(jnp.int32, (size, width), 1)
    own = r2 // chunk == c2 // hd
    groups = range(D_MODEL // width)
    cols_of = lambda gi: slice(gi * width, (gi + 1) * width)
    tile = lambda x: jnp.concatenate([x] * group, axis=0)
    stack = lambda x: jnp.where(own, tile(x), 0.0)
    unstack = lambda x: sum(x[hh * chunk:(hh + 1) * chunk] for hh in range(group))

    tr = lax.broadcasted_iota(jnp.int32, (chunk, chunk), 0)
    tc = lax.broadcasted_iota(jnp.int32, (chunk, chunk), 1)
    lw_all = lw_ref[...]
    cw_all = _dot_exact_lhs(jnp.where(tc <= tr, 1.0, 0.0), lw_all)
    tot_all = jnp.sum(lw_all, axis=0, keepdims=True)
    a_all = a_ref[...]
    k_all = k_ref[...]
    v_all = v_ref[...]
    e_neg = jnp.exp(-cw_all)
    e_end = jnp.exp(tot_all - cw_all)
    neg_e_prev = -jnp.exp(cw_all - lw_all)
    r_cw = r_ref[...] * jnp.exp(cw_all)
    k2_all = k_all * (1.0 + (a_all - 1.0) * ka_ref[...])
    kk_raw = k_all * kk_ref[...]
    rk_all = r_ref[...] * k2_all * rk_ref[...]
    kt_all, k_end_all = k2_all * e_neg, k2_all * e_end

    kk = [stack(kk_raw[:, cols_of(gi)]) for gi in groups]
    kk = [x * lax.rsqrt(jnp.sum(x * x, axis=-1, keepdims=True) + NORM_EPS) for x in kk]
    b = [unstack(kk[gi]) * a_all[:, cols_of(gi)] for gi in groups]
    at = [kk[gi] * tile(neg_e_prev[:, cols_of(gi)]) for gi in groups]
    rt = [stack(r_cw[:, cols_of(gi)]) for gi in groups]
    v = [stack(v_all[:, cols_of(gi)]) for gi in groups]
    bt = [tile(b[gi] * e_neg[:, cols_of(gi)]) for gi in groups]
    kt = [tile(kt_all[:, cols_of(gi)]) for gi in groups]
    b_end = [stack(b[gi] * e_end[:, cols_of(gi)]) for gi in groups]
    k_end = [stack(k_end_all[:, cols_of(gi)]) for gi in groups]
    a_ab = [jnp.where(strict, _dot_nt(at[gi], bt[gi]), 0.0) for gi in groups]
    a_ak = [jnp.where(strict, _dot_nt(at[gi], kt[gi]), 0.0) for gi in groups]
    a_rb = [jnp.where(lower, _dot_nt(rt[gi], bt[gi]), 0.0) for gi in groups]
    a_rk = [jnp.where(lower, _dot_nt(rt[gi], kt[gi]), 0.0) for gi in groups]
    t_inv = _inv_i_minus_many(a_ab, size, chunk)
    s = [s_ref[gi] for gi in groups]
    rhs = [_dot_nt(at[gi], s[gi]) + _dot(a_ak[gi], v[gi]) for gi in groups]
    o_past = [_dot_nt(rt[gi], s[gi]) + _dot(a_rk[gi], v[gi]) for gi in groups]
    u = [_dot(t_inv[gi], rhs[gi]) for gi in groups]
    o = [o_past[gi] + _dot(a_rb[gi], u[gi]) for gi in groups]
    for gi in groups:
        s_ref[gi] = (s[gi] * jnp.exp(tot_all[:, cols_of(gi)]) + _dot_tn(u[gi], b_end[gi])
                     + _dot_tn(v[gi], k_end[gi]))
    for gi in groups:
        sl = cols_of(gi)
        mu = jnp.sum(o[gi], axis=-1, keepdims=True) * (1.0 / hd)
        oc = jnp.where(own, o[gi] - mu, 0.0)
        var = jnp.sum(oc * oc, axis=-1, keepdims=True) * (1.0 / hd)
        bonus = jnp.sum(stack(rk_all[:, sl]), axis=-1, keepdims=True) * v[gi]
        out = unstack(oc * lax.rsqrt(var + RWKV_GN_EPS)) * lg_ref[:, sl] + lb_ref[:, sl] + unstack(bonus)
        o_ref[:, sl] = (out * g_ref[:, sl]).astype(o_ref.dtype)

    @pl.when(c == pl.num_programs(1) - 1)
    def _():
        s_out_ref[0] = s_ref[...]


def _rwkv(r, k, v, lw, a, g, row0, n_seq, seq_len, chunk, state0, k_k, k_a, r_k, lnx_g, lnx_b, group=4):
    d = D_MODEL
    n_groups = RWKV_HEADS // group
    width = group * RWKV_HEAD_DIM
    n_chunks = seq_len // chunk
    blk0 = row0 // chunk
    tok = pl.BlockSpec((chunk, d), lambda s, c: (blk0 + s * n_chunks + c, 0))
    vec = pl.BlockSpec((1, d), lambda s, c: (0, 0))
    s5 = state0.reshape(n_seq, n_groups, group, RWKV_HEAD_DIM, RWKV_HEAD_DIM)
    eye = jnp.eye(group, dtype=F32)
    s_bd = (s5[:, :, :, :, None, :] * eye[None, None, :, None, :, None]).reshape(n_seq, n_groups, width, width)
    state_spec = pl.BlockSpec((1, n_groups, width, width), lambda s, c: (s, 0, 0, 0))
    o, s_new = pl.pallas_call(
        functools.partial(_rwkv_kernel, chunk=chunk, group=group),
        grid=(n_seq, n_chunks),
        in_specs=[tok] * 6 + [vec] * 5 + [state_spec],
        out_specs=[pl.BlockSpec((chunk, d), lambda s, c: (s * n_chunks + c, 0)), state_spec],
        out_shape=[jax.ShapeDtypeStruct((n_seq * seq_len, d), BF16),
                   jax.ShapeDtypeStruct(s_bd.shape, F32)],
        scratch_shapes=[pltpu.VMEM((n_groups, width, width), F32)],
        compiler_params=_params("parallel", "arbitrary"),
        name="rwkv_wkv",
    )(r, k, v, lw, a, g, k_k.reshape(1, d), k_a.reshape(1, d), r_k.reshape(1, d),
      lnx_g.reshape(1, d), lnx_b.reshape(1, d), s_bd)
    s6 = s_new.reshape(n_seq, n_groups, group, RWKV_HEAD_DIM, group, RWKV_HEAD_DIM)
    diag = jnp.stack([s6[:, :, h, :, h, :] for h in range(group)], axis=2)
    return o, diag.reshape(state0.shape)


def _forward(x_prompt, x_sample, cache_k, cache_v, page_table, state_gdn_conv, state_gdn,
             state_rwkv_shift, state_rwkv, p, *, tm, tm_proj, sb_tq, sb_tk, pages_per_step,
             chunk_prompt, gdn_chunk_sample, rwkv_chunk_sample, ffn_tf, tm_moe, moe_rows, moe_strip, gdn_sub):
    bp, t_p, d = x_prompt.shape
    bs, t_s, _ = x_sample.shape
    assert bp == 1
    n_p, n_s = bp * t_p, bs * t_s
    x = jnp.concatenate([x_prompt.reshape(n_p, d), x_sample.reshape(n_s, d)], axis=0)

    w_in = p['w_in0']
    cut = 3 * SB_WIDTH + 3 * GDN_WIDTH
    w_pad = jnp.concatenate(
        [w_in[:, :cut], w_in[:, cut + 2 * GDN_HEADS:], w_in[:, cut:cut + 2 * GDN_HEADS],
         jnp.zeros((d, COL_BLOCK - 2 * GDN_HEADS), F32)], axis=1).astype(BF16)
    y = _matmul(x, w_pad, tm, COL_BLOCK)
    q = y[:, :SB_WIDTH]
    k_rows = y[:, SB_WIDTH:2 * SB_WIDTH]
    v_rows = y[:, 2 * SB_WIDTH:3 * SB_WIDTH]
    gdn_rows = y[:, 3 * SB_WIDTH:3 * SB_WIDTH + 3 * GDN_WIDTH]

    o_sb_p = _sb_prompt(q[:n_p], k_rows[:n_p], v_rows[:n_p], p['sb_bias'], sb_tq, sb_tk)
    shape_s = (bs, t_s, SB_WIDTH)
    o_sb_s = _sb_sample(q[n_p:].reshape(shape_s), k_rows[n_p:].reshape(shape_s), v_rows[n_p:].reshape(shape_s),
                        cache_k, cache_v, page_table, p['sb_bias'], pages_per_step)
    o_sb = jnp.concatenate([o_sb_p, o_sb_s.reshape(n_s, SB_WIDTH)], axis=0)

    gdn_args = (p['gdn_conv_w'], p['gdn_a_log'], p['gdn_dt_bias'], p['gdn_norm_w'])
    o_gdn_p, gdn_state_p = _gdn(y, 0, bp, t_p, chunk_prompt, gdn_sub,
                                jnp.zeros((bp, GDN_CONV - 1, 3 * GDN_WIDTH), F32),
                                jnp.zeros((bp,) + state_gdn.shape[1:], F32), *gdn_args)
    o_gdn_s, gdn_state_s = _gdn(y, n_p, bs, t_s, gdn_chunk_sample, 1, state_gdn_conv, state_gdn, *gdn_args)
    o_gdn = jnp.concatenate([o_gdn_p, o_gdn_s], axis=0)
    conv_p = gdn_rows[:n_p].reshape(bp, t_p, -1)[:, t_p - (GDN_CONV - 1):]
    conv_s = gdn_rows[n_p:].reshape(bs, t_s, -1)[:, t_s - (GDN_CONV - 1):]

    w_out = p['w_out0'].astype(BF16)
    x = _proj_ln([o_sb, o_gdn], [w_out[:SB_WIDTH], w_out[SB_WIDTH:]], x,
                 p['ln_gamma'][0, 0], p['ln_beta'][0, 0], tm, "out_proj_ln")
    x = _ffn_ln(x, p['ffn_gate'].astype(BF16), p['ffn_up'].astype(BF16), p['ffn_down'].astype(BF16),
                p['ln_gamma'][0, 1], p['ln_beta'][0, 1], tm, ffn_tf)

    x_p = x[:n_p].reshape(bp, t_p, d)
    x_s = x[n_p:].reshape(bs, t_s, d)
    prev_p = jnp.concatenate([jnp.zeros((bp, 1, d), F32), x_p[:, :-1]], axis=1)
    prev_s = jnp.concatenate([state_rwkv_shift[:, None, :], x_s[:, :-1]], axis=1)
    x_prev = jnp.concatenate([prev_p.reshape(n_p, d), prev_s.reshape(n_s, d)], axis=0)
    bf = lambda name: p[name].astype(BF16)
    r, k, v, lw, a, g = _rwkv_proj(
        x, x_prev, p['rwkv_mix'], bf('rwkv_w_r'), bf('rwkv_w_k'), bf('rwkv_w_v'), bf('rwkv_w1'), bf('rwkv_w2'),
        bf('rwkv_a1'), bf('rwkv_a2'), bf('rwkv_g1'), bf('rwkv_g2'),
        p['rwkv_w0'].reshape(1, d), p['rwkv_a0'].reshape(1, d), tm_proj)
    rwkv_args = (p['rwkv_k_k'], p['rwkv_k_a'], p['rwkv_r_k'], p['rwkv_lnx_g'], p['rwkv_lnx_b'])
    o_p, rwkv_state_p = _rwkv(r, k, v, lw, a, g, 0, bp, t_p, chunk_prompt,
                              jnp.zeros((bp,) + state_rwkv.shape[1:], F32), *rwkv_args)
    o_s, rwkv_state_s = _rwkv(r, k, v, lw, a, g, n_p, bs, t_s, rwkv_chunk_sample, state_rwkv, *rwkv_args)
    x1 = _proj_ln([jnp.concatenate([o_p, o_s], axis=0)], [bf('rwkv_w_o')], x,
                  p['ln_gamma'][1, 0], p['ln_beta'][1, 0], tm, "rwkv_out_ln")
    w_router = jnp.pad(p['moe_router'], ((0, 0), (0, 128 - N_EXPERTS)))
    out = _moe_ln(x1, bf('moe_gate'), bf('moe_up'), bf('moe_down'), p['ln_gamma'][1, 1], p['ln_beta'][1, 1],
                  w_router, tm_moe, ffn_tf, moe_rows, moe_strip)

    heads = lambda rows, b_, t_: rows.reshape(b_, t_, SB_HEADS, SB_HEAD_DIM)
    return (out[:n_p].reshape(bp, t_p, d), out[n_p:].reshape(bs, t_s, d),
            heads(k_rows[:n_p], bp, t_p), heads(v_rows[:n_p], bp, t_p), conv_p, gdn_state_p,
            x_p[:, -1], rwkv_state_p,
            heads(k_rows[n_p:], bs, t_s), heads(v_rows[n_p:], bs, t_s), conv_s, gdn_state_s,
            x_s[:, -1], rwkv_state_s)


def kernel(x_prompt, x_sample, cache_k, cache_v, page_table, state_gdn_conv, state_gdn, state_rwkv_shift, state_rwkv, w_in0, sb_bias, gdn_conv_w, gdn_a_log, gdn_dt_bias, gdn_norm_w, w_out0, ffn_gate, ffn_up, ffn_down, rwkv_mix, rwkv_w_r, rwkv_w_k, rwkv_w_v, rwkv_w0, rwkv_w1, rwkv_w2, rwkv_a0, rwkv_a1, rwkv_a2, rwkv_g1, rwkv_g2, rwkv_k_k, rwkv_k_a, rwkv_r_k, rwkv_lnx_g, rwkv_lnx_b, rwkv_w_o, moe_router, moe_gate, moe_up, moe_down, ln_gamma, ln_beta):
    p = dict(w_in0=w_in0, sb_bias=sb_bias, gdn_conv_w=gdn_conv_w, gdn_a_log=gdn_a_log, gdn_dt_bias=gdn_dt_bias,
             gdn_norm_w=gdn_norm_w, w_out0=w_out0, ffn_gate=ffn_gate, ffn_up=ffn_up, ffn_down=ffn_down,
             rwkv_mix=rwkv_mix, rwkv_w_r=rwkv_w_r, rwkv_w_k=rwkv_w_k, rwkv_w_v=rwkv_w_v,
             rwkv_w0=rwkv_w0, rwkv_w1=rwkv_w1, rwkv_w2=rwkv_w2, rwkv_a0=rwkv_a0, rwkv_a1=rwkv_a1,
             rwkv_a2=rwkv_a2, rwkv_g1=rwkv_g1, rwkv_g2=rwkv_g2, rwkv_k_k=rwkv_k_k, rwkv_k_a=rwkv_k_a,
             rwkv_r_k=rwkv_r_k, rwkv_lnx_g=rwkv_lnx_g, rwkv_lnx_b=rwkv_lnx_b, rwkv_w_o=rwkv_w_o,
             moe_router=moe_router, moe_gate=moe_gate, moe_up=moe_up, moe_down=moe_down,
             ln_gamma=ln_gamma, ln_beta=ln_beta)
    return _forward(x_prompt, x_sample, cache_k, cache_v, page_table, state_gdn_conv, state_gdn,
                    state_rwkv_shift, state_rwkv, p, tm=640, tm_proj=320, sb_tq=512, sb_tk=256,
                    pages_per_step=16, chunk_prompt=64, gdn_chunk_sample=8, rwkv_chunk_sample=8, ffn_tf=1408,
                    tm_moe=1280, moe_rows=384, moe_strip=256, gdn_sub=4)
```

```python
import functools
import math

import jax
import jax.numpy as jnp
import numpy as np
from jax import lax
from jax.experimental import pallas as pl
from jax.experimental.pallas import tpu as pltpu

F32 = jnp.float32
BF16 = jnp.bfloat16

D_MODEL = 1024
SB_HEADS = 8
SB_HEAD_DIM = 64
SB_WIDTH = SB_HEADS * SB_HEAD_DIM
GDN_HEADS = 4
GDN_HEAD_DIM = 128
GDN_WIDTH = GDN_HEADS * GDN_HEAD_DIM
GDN_CONV = 4
RWKV_HEAD_DIM = 64
RWKV_HEADS = D_MODEL // RWKV_HEAD_DIM
RWKV_GN_EPS = 64e-5
N_EXPERTS = 8
DEPTH = 2
DEEPNORM_ALPHA = (2 * DEPTH) ** 0.25
LN_EPS = 1e-5
NORM_EPS = 1e-6
LOG2E = math.log2(math.e)

IN0_COLS = 4096
COL_BLOCK = 512
VMEM_LIMIT_BYTES = 56 * 1024 * 1024

_NT = (((1,), (1,)), ((), ()))
_TN = (((0,), (0,)), ((), ()))


def _params(*sem):
    return pltpu.CompilerParams(dimension_semantics=sem, vmem_limit_bytes=VMEM_LIMIT_BYTES)


def _dot(a, b):
    return jnp.dot(a.astype(BF16), b.astype(BF16), preferred_element_type=F32)


def _dot_nt(a, b):
    return lax.dot_general(a.astype(BF16), b.astype(BF16), _NT, preferred_element_type=F32)


def _dot_tn(a, b):
    return lax.dot_general(a.astype(BF16), b.astype(BF16), _TN, preferred_element_type=F32)


def _dot_exact_lhs(a01, x):
    a = a01.astype(BF16)
    x1 = x.astype(BF16)
    r1 = x - x1.astype(F32)
    x2 = r1.astype(BF16)
    x3 = (r1 - x2.astype(F32)).astype(BF16)
    out = jnp.dot(a, x1, preferred_element_type=F32)
    out = out + jnp.dot(a, x2, preferred_element_type=F32)
    return out + jnp.dot(a, x3, preferred_element_type=F32)


def _dot_exact_rhs(x, b01):
    b = b01.astype(BF16)
    x1 = x.astype(BF16)
    r1 = x - x1.astype(F32)
    x2 = r1.astype(BF16)
    x3 = (r1 - x2.astype(F32)).astype(BF16)
    out = jnp.dot(x1, b, preferred_element_type=F32)
    out = out + jnp.dot(x2, b, preferred_element_type=F32)
    return out + jnp.dot(x3, b, preferred_element_type=F32)


def _softplus(z):
    return jnp.maximum(z, 0.0) + jnp.log1p(jnp.exp(-jnp.abs(z)))


def _sigmoid(z):
    return 1.0 / (1.0 + jnp.exp(-z))


def _silu(z):
    return z * _sigmoid(z)


def _layer_norm(x, g, b):
    mu = jnp.mean(x, axis=-1, keepdims=True)
    xc = x - mu
    var = jnp.mean(xc * xc, axis=-1, keepdims=True)
    return xc * lax.rsqrt(var + LN_EPS) * g + b


def _inv_i_minus(n, size, block):
    return _inv_i_minus_many([n], size, block)[0]


def _inv_i_minus_many(ns, size, block):
    rows = lax.broadcasted_iota(jnp.int32, (size, size), 0)
    cols = lax.broadcasted_iota(jnp.int32, (size, size), 1)
    eye = jnp.where(rows == cols, 1.0, 0.0)
    ps = [eye + n for n in ns]
    ys = list(ns)
    for _ in range(max(0, int(math.ceil(math.log2(block))) - 1)):
        ys = [_dot(y, y) for y in ys]
        ps = [p + _dot(p, y) for p, y in zip(ps, ys)]
    return ps


def _matmul_kernel(x_ref, w_ref, o_ref, *, tn):
    xb = x_ref[...].astype(BF16)
    for j in range(w_ref.shape[1] // tn):
        o_ref[:, j * tn:(j + 1) * tn] = jnp.dot(xb, w_ref[:, j * tn:(j + 1) * tn], preferred_element_type=F32)


def _matmul(x, w, tm, tn):
    n, k = x.shape
    m = w.shape[1]
    return pl.pallas_call(
        functools.partial(_matmul_kernel, tn=tn),
        grid=(n // tm,),
        in_specs=[pl.BlockSpec((tm, k), lambda i: (i, 0)),
                  pl.BlockSpec((k, m), lambda i: (0, 0))],
        out_specs=pl.BlockSpec((tm, m), lambda i: (i, 0)),
        out_shape=jax.ShapeDtypeStruct((n, m), F32),
        compiler_params=_params("parallel"),
        name="in_proj",
    )(x, w)


def _proj_ln_kernel(*refs, n_in):
    a_refs = refs[:n_in]
    w_refs = refs[n_in:2 * n_in]
    x_ref, g_ref, b_ref, o_ref = refs[2 * n_in:]
    h = jnp.dot(a_refs[0][...], w_refs[0][...], preferred_element_type=F32)
    for a_ref, w_ref in zip(a_refs[1:], w_refs[1:]):
        h = h + jnp.dot(a_ref[...], w_ref[...], preferred_element_type=F32)
    o_ref[...] = _layer_norm(DEEPNORM_ALPHA * x_ref[...] + h, g_ref[...], b_ref[...])


def _proj_ln(acts, weights, x, gamma, beta, tm, name):
    n, d = x.shape
    n_in = len(acts)
    in_specs = [pl.BlockSpec((tm, a.shape[1]), lambda i: (i, 0)) for a in acts]
    in_specs += [pl.BlockSpec(w.shape, lambda i: (0, 0)) for w in weights]
    in_specs += [pl.BlockSpec((tm, d), lambda i: (i, 0)),
                 pl.BlockSpec((1, d), lambda i: (0, 0)),
                 pl.BlockSpec((1, d), lambda i: (0, 0))]
    return pl.pallas_call(
        functools.partial(_proj_ln_kernel, n_in=n_in),
        grid=(n // tm,),
        in_specs=in_specs,
        out_specs=pl.BlockSpec((tm, d), lambda i: (i, 0)),
        out_shape=jax.ShapeDtypeStruct((n, d), F32),
        compiler_params=_params("parallel"),
        name=name,
    )(*acts, *weights, x, gamma.reshape(1, d), beta.reshape(1, d))


def _top2_gates(logits):
    lane = lax.broadcasted_iota(jnp.int32, logits.shape, 1).astype(F32)
    big = float(logits.shape[1])
    lg = jnp.where(lane < N_EXPERTS, logits, -jnp.inf)
    m1 = jnp.max(lg, axis=-1, keepdims=True)
    i1 = jnp.min(jnp.where(lg == m1, lane, big), axis=-1, keepdims=True)
    lg2 = jnp.where(lane == i1, -jnp.inf, lg)
    m2 = jnp.max(lg2, axis=-1, keepdims=True)
    i2 = jnp.min(jnp.where(lg2 == m2, lane, big), axis=-1, keepdims=True)
    e2 = jnp.exp(m2 - m1)
    den = 1.0 + e2
    gates = jnp.where(lane == i1, 1.0 / den, 0.0) + jnp.where(lane == i2, e2 / den, 0.0)
    chosen = jnp.where(jnp.logical_or(lane == i1, lane == i2), 1.0, 0.0)
    return gates, chosen


def _ffn_ln_kernel(x_ref, wg_ref, wu_ref, wd_ref, g_ref, b_ref, o_ref, acc_ref):
    f = pl.program_id(1)

    @pl.when(f == 0)
    def _():
        acc_ref[...] = jnp.zeros_like(acc_ref)

    xb = x_ref[...].astype(BF16)
    hg = jnp.dot(xb, wg_ref[...], preferred_element_type=F32)
    hu = jnp.dot(xb, wu_ref[...], preferred_element_type=F32)
    acc_ref[...] += jnp.dot((_silu(hg) * hu).astype(BF16), wd_ref[...], preferred_element_type=F32)

    @pl.when(f == pl.num_programs(1) - 1)
    def _():
        o_ref[...] = _layer_norm(DEEPNORM_ALPHA * x_ref[...] + acc_ref[...], g_ref[...], b_ref[...])


def _ffn_ln(x, w_gate, w_up, w_down, gamma, beta, tm, tf):
    n, d = x.shape
    d_ff = w_gate.shape[1]
    return pl.pallas_call(
        _ffn_ln_kernel,
        grid=(n // tm, d_ff // tf),
        in_specs=[pl.BlockSpec((tm, d), lambda i, f: (i, 0)),
                  pl.BlockSpec((d, tf), lambda i, f: (0, f)),
                  pl.BlockSpec((d, tf), lambda i, f: (0, f)),
                  pl.BlockSpec((tf, d), lambda i, f: (f, 0)),
                  pl.BlockSpec((1, d), lambda i, f: (0, 0)),
                  pl.BlockSpec((1, d), lambda i, f: (0, 0))],
        out_specs=pl.BlockSpec((tm, d), lambda i, f: (i, 0)),
        out_shape=jax.ShapeDtypeStruct((n, d), F32),
        scratch_shapes=[pltpu.VMEM((tm, d), F32)],
        compiler_params=_params("parallel", "arbitrary"),
        name="ffn_ln",
    )(x, w_gate, w_up, w_down, gamma.reshape(1, d), beta.reshape(1, d))


def _moe_kernel(x_ref, wg_ref, wu_ref, wd_ref, g_ref, b_ref, wr_ref, o_ref,
                xb_ref, gate_ref, key_ref, keyt_ref, xe_ref, ye_ref, *, rows, strip):
    tm, d = x_ref.shape
    e = pl.program_id(1)
    f = pl.program_id(2)
    n_f = pl.num_programs(2)
    n_lane = gate_ref.shape[1]

    @pl.when(jnp.logical_and(e == 0, f == 0))
    def _():
        x = x_ref[...]
        xb_ref[...] = x.astype(BF16)
        logits = jnp.dot(x, wr_ref[...], preferred_element_type=F32, precision=lax.Precision.HIGHEST)
        gates, chosen = _top2_gates(logits)
        gate_ref[...] = gates
        tr = lax.broadcasted_iota(jnp.int32, (tm, tm), 0)
        tc = lax.broadcasted_iota(jnp.int32, (tm, tm), 1)
        before = jnp.where(tc < tr, 1.0, 0.0).astype(BF16)
        rank = jnp.dot(before, chosen.astype(BF16), preferred_element_type=F32)
        key = jnp.where(chosen > 0.0, rank, -1.0)
        key_ref[...] = key
        keyt_ref[...] = key.T
        o_ref[...] = jnp.zeros_like(o_ref)

    lane = lax.broadcasted_iota(jnp.int32, (tm, n_lane), 1)
    key_col = jnp.sum(jnp.where(lane == e, key_ref[...], 0.0), axis=-1, keepdims=True)
    count = jnp.sum(jnp.where(key_col >= 0.0, 1.0, 0.0)).astype(jnp.int32)
    n_chunks = (count + rows - 1) // rows

    @pl.when(f == 0)
    def _():
        key_row = keyt_ref[pl.ds(e, 1), :]

        def gather(c, carry):
            slot = (c * rows + lax.broadcasted_iota(jnp.int32, (rows, tm), 0)).astype(F32)
            sel = jnp.where(key_row == slot, 1.0, 0.0).astype(BF16)
            start = pl.multiple_of(c * rows, rows)
            xe_ref[pl.ds(start, rows), :] = jnp.dot(sel, xb_ref[...], preferred_element_type=F32).astype(BF16)
            return carry

        lax.fori_loop(0, n_chunks, gather, 0)

    def expert(c, carry):
        start = pl.multiple_of(c * rows, rows)
        xe = xe_ref[pl.ds(start, rows), :]
        hg = jnp.dot(xe, wg_ref[0], preferred_element_type=F32)
        hu = jnp.dot(xe, wu_ref[0], preferred_element_type=F32)
        y = jnp.dot((_silu(hg) * hu).astype(BF16), wd_ref[0], preferred_element_type=F32)

        @pl.when(f == 0)
        def _():
            ye_ref[pl.ds(start, rows), :] = y

        @pl.when(f > 0)
        def _():
            ye_ref[pl.ds(start, rows), :] += y

        return carry

    lax.fori_loop(0, n_chunks, expert, 0)

    @pl.when(f == n_f - 1)
    def _():
        gate_col = jnp.sum(jnp.where(lane == e, gate_ref[...], 0.0), axis=-1, keepdims=True)

        def scatter(c, carry):
            start = pl.multiple_of(c * rows, rows)
            y = ye_ref[pl.ds(start, rows), :].astype(BF16)
            for s in range(tm // strip):
                tok = slice(s * strip, (s + 1) * strip)
                slot = (c * rows + lax.broadcasted_iota(jnp.int32, (strip, rows), 1)).astype(F32)
                sel = jnp.where(key_col[tok] == slot, 1.0, 0.0).astype(BF16)
                o_ref[tok, :] += gate_col[tok] * jnp.dot(sel, y, preferred_element_type=F32)
            return carry

        lax.fori_loop(0, n_chunks, scatter, 0)

    @pl.when(jnp.logical_and(e == pl.num_programs(1) - 1, f == n_f - 1))
    def _():
        for s in range(tm // strip):
            tok = slice(s * strip, (s + 1) * strip)
            o_ref[tok, :] = _layer_norm(DEEPNORM_ALPHA * x_ref[tok, :] + o_ref[tok, :], g_ref[...], b_ref[...])


def _moe_ln(x, w_gate, w_up, w_down, gamma, beta, w_router, tm, tf, rows, strip):
    n, d = x.shape
    n_e, _, d_ff = w_gate.shape
    assert tm % strip == 0
    cap = -(-tm // rows) * rows
    once = dict(pipeline_mode=pl.Buffered(1))
    return pl.pallas_call(
        functools.partial(_moe_kernel, rows=rows, strip=strip),
        grid=(n // tm, n_e, d_ff // tf),
        in_specs=[pl.BlockSpec((tm, d), lambda i, e, f: (i, 0), **once),
                  pl.BlockSpec((1, d, tf), lambda i, e, f: (e, 0, f)),
                  pl.BlockSpec((1, d, tf), lambda i, e, f: (e, 0, f)),
                  pl.BlockSpec((1, tf, d), lambda i, e, f: (e, f, 0)),
                  pl.BlockSpec((1, d), lambda i, e, f: (0, 0)),
                  pl.BlockSpec((1, d), lambda i, e, f: (0, 0)),
                  pl.BlockSpec(w_router.shape, lambda i, e, f: (0, 0))],
        out_specs=pl.BlockSpec((tm, d), lambda i, e, f: (i, 0)),
        out_shape=jax.ShapeDtypeStruct((n, d), F32),
        scratch_shapes=[pltpu.VMEM((tm, d), BF16),
                        pltpu.VMEM((tm, w_router.shape[1]), F32),
                        pltpu.VMEM((tm, w_router.shape[1]), F32),
                        pltpu.VMEM((w_router.shape[1], tm), F32),
                        pltpu.VMEM((cap, d), BF16),
                        pltpu.VMEM((cap, d), F32)],
        compiler_params=_params("parallel", "arbitrary", "arbitrary"),
        name="moe_ln",
    )(x, w_gate, w_up, w_down, gamma.reshape(1, d), beta.reshape(1, d), w_router)


def _rwkv_proj_kernel(x_ref, xp_ref, mix_ref, wr_ref, wk_ref, wv_ref, w1_ref, w2_ref, a1_ref, a2_ref,
                      g1_ref, g2_ref, w0_ref, a0_ref, r_ref, k_ref, v_ref, lw_ref, a_ref, g_ref):
    x = x_ref[...]
    xx = xp_ref[...] - x
    mixed = lambda i: (x + xx * mix_ref[i:i + 1, :]).astype(BF16)
    r_ref[...] = jnp.dot(mixed(0), wr_ref[...], preferred_element_type=F32)
    w_lora = _dot(jnp.tanh(jnp.dot(mixed(1), w1_ref[...], preferred_element_type=F32)), w2_ref[...])
    lw_ref[...] = -jnp.exp(-_softplus(-(w0_ref[...] + w_lora)) - 0.5)
    k_ref[...] = jnp.dot(mixed(2), wk_ref[...], preferred_element_type=F32)
    v_ref[...] = jnp.dot(mixed(3), wv_ref[...], preferred_element_type=F32)
    a_lora = _dot(jnp.dot(mixed(4), a1_ref[...], preferred_element_type=F32), a2_ref[...])
    a_ref[...] = _sigmoid(a0_ref[...] + a_lora)
    g_ref[...] = _dot(_sigmoid(jnp.dot(mixed(5), g1_ref[...], preferred_element_type=F32)), g2_ref[...])


def _rwkv_proj(x, x_prev, mix, w_r, w_k, w_v, w1, w2, a1, a2, g1, g2, w0, a0, tm):
    n, d = x.shape
    tok = pl.BlockSpec((tm, d), lambda i: (i, 0))
    full = lambda a: pl.BlockSpec(a.shape, lambda i: (0, 0))
    consts = [mix, w_r, w_k, w_v, w1, w2, a1, a2, g1, g2, w0, a0]
    return pl.pallas_call(
        _rwkv_proj_kernel,
        grid=(n // tm,),
        in_specs=[tok, tok] + [full(c) for c in consts],
        out_specs=[tok] * 6,
        out_shape=[jax.ShapeDtypeStruct((n, d), F32)] * 6,
        compiler_params=_params("parallel"),
        name="rwkv_proj",
    )(x, x_prev, *consts)


def _suffix_sum_rows8(x):
    row = lax.broadcasted_iota(jnp.int32, x.shape, 0)
    for sh in (1, 2, 4):
        shifted = pltpu.roll(x, 8 - sh, axis=0)
        x = x + jnp.where(row < 8 - sh, shifted, 0.0)
    return x


def _sb_block(z_ref, r_ref, a_ref, carry, visible):
    tk, tq = z_ref.shape
    nsub = tk // 8
    run = jnp.zeros((8, tq), F32)
    for s in reversed(range(nsub)):
        rows = slice(s * 8, (s + 1) * 8)
        z = z_ref[rows, :]
        neg_abs = pltpu.bitcast(pltpu.bitcast(z, jnp.uint32) | jnp.uint32(0x80000000), F32)
        sp = jnp.maximum(z, 0.0) + jnp.log2(1.0 + jnp.exp2(neg_abs))
        vis = visible(s * 8, 8)
        if vis is None:
            run = run + sp
            r_ref[rows, :] = z - run
        else:
            r_ref[rows, :] = (z - sp) - run
            run = run + jnp.where(vis, sp, 0.0)
    incl = _suffix_sum_rows8(run)
    offset = incl - run + carry
    offset2 = jnp.concatenate([offset, offset], axis=0)
    for s in range(nsub // 2):
        rows = slice(s * 16, (s + 1) * 16)
        a = jnp.exp2(r_ref[rows, :] - offset2)
        vis = visible(s * 16, 16)
        if vis is not None:
            a = jnp.where(vis, a, 0.0)
        a_ref[rows, :] = a.astype(BF16)
    return carry + incl[0:1]


def _sb_prompt_kernel(q_ref, k_ref, vt_ref, o_ref, z0_ref, z1_ref, a0_ref, a1_ref, *, tq, tk):
    assert tq == 2 * tk
    i = pl.program_id(1)
    q = q_ref[0]
    nsub = tk // 8

    def scores(j):
        start = pl.multiple_of(jnp.maximum(j, 0) * tk, tk)
        return lax.dot_general(k_ref[0, pl.ds(start, tk), :], q, _NT, preferred_element_type=F32)

    def weighted_v(j, a_ref):
        start = pl.multiple_of(j * tk, tk)
        return jnp.dot(vt_ref[0, :, pl.ds(start, tk)], a_ref[...], preferred_element_type=F32)

    def weights(z_ref, a_ref, j, carry, masked):
        def visible(first_row, n_rows):
            if not masked:
                return None
            row = first_row + lax.broadcasted_iota(jnp.int32, (n_rows, tq), 0)
            col = lax.broadcasted_iota(jnp.int32, (n_rows, tq), 1)
            k_pos = j * tk + (row % 8) * nsub + row // 8
            return k_pos < i * tq + col
        return _sb_block(z_ref, z_ref, a_ref, carry, visible)

    newest = 2 * i + 1
    acc = jnp.zeros((SB_HEAD_DIM, tq), F32)
    carry = jnp.zeros((1, tq), F32)
    z0_ref[...] = scores(newest)
    z1_ref[...] = scores(newest - 1)
    carry = weights(z0_ref, a0_ref, newest, carry, True)
    z0_ref[...] = scores(newest - 2)
    acc = acc + weighted_v(newest, a0_ref)
    carry = weights(z1_ref, a1_ref, newest - 1, carry, True)

    def body(t, state):
        acc, carry = state
        blk = newest - 2 - 2 * t
        z1_ref[...] = scores(blk - 1)
        acc = acc + weighted_v(blk + 1, a1_ref)
        carry = weights(z0_ref, a0_ref, blk, carry, False)
        z0_ref[...] = scores(blk - 2)
        acc = acc + weighted_v(blk, a0_ref)
        carry = weights(z1_ref, a1_ref, blk - 1, carry, False)
        return acc, carry

    acc, carry = lax.fori_loop(0, i, body, (acc, carry))
    acc = acc + weighted_v(0, a1_ref)
    o_ref[0] = acc.astype(o_ref.dtype)


def _bf16_pieces(x, n):
    out = []
    for _ in range(n):
        piece = x.astype(BF16)
        out.append(piece)
        x = x - piece.astype(F32)
    return out


def _sb_prompt(q, k, v, bias, tq, tk):
    t = q.shape[0]
    nblk, nsub = t // tk, tk // 8
    pad = 128 - SB_HEAD_DIM
    qh = (q * (SB_HEAD_DIM ** -0.5 * LOG2E)).astype(BF16).reshape(t, SB_HEADS, SB_HEAD_DIM).transpose(1, 0, 2)
    bias_cols = jnp.stack(_bf16_pieces(bias.astype(F32) * LOG2E, 3), axis=-1)
    q_extra = jnp.pad(bias_cols, ((0, 0), (0, pad - 3)))[:, None, :]
    q_aug = jnp.concatenate([qh, jnp.broadcast_to(q_extra, (SB_HEADS, t, pad))], axis=-1)
    k5 = k.astype(BF16).reshape(nblk, 8, nsub, SB_HEADS, SB_HEAD_DIM)
    kh = k5.transpose(3, 0, 2, 1, 4).reshape(SB_HEADS, t, SB_HEAD_DIM)
    k_extra = jnp.pad(jnp.ones((3,), BF16), (0, pad - 3))
    k_aug = jnp.concatenate([kh, jnp.broadcast_to(k_extra, (SB_HEADS, t, pad))], axis=-1)
    v5 = v.astype(BF16).reshape(nblk, 8, nsub, SB_HEADS, SB_HEAD_DIM)
    vt = v5.transpose(3, 4, 0, 2, 1).reshape(SB_HEADS, SB_HEAD_DIM, t)
    out_t = pl.pallas_call(
        functools.partial(_sb_prompt_kernel, tq=tq, tk=tk),
        grid=(SB_HEADS, t // tq),
        in_specs=[pl.BlockSpec((1, tq, 128), lambda h, i: (h, i, 0)),
                  pl.BlockSpec((1, t, 128), lambda h, i: (h, 0, 0)),
                  pl.BlockSpec((1, SB_HEAD_DIM, t), lambda h, i: (h, 0, 0))],
        out_specs=pl.BlockSpec((1, SB_HEAD_DIM, tq), lambda h, i: (h, 0, i)),
        out_shape=jax.ShapeDtypeStruct((SB_HEADS, SB_HEAD_DIM, t), BF16),
        scratch_shapes=[pltpu.VMEM((tk, tq), F32), pltpu.VMEM((tk, tq), F32),
                        pltpu.VMEM((tk, tq), BF16), pltpu.VMEM((tk, tq), BF16)],
        compiler_params=_params("parallel", "parallel"),
        name="sb_prompt",
    )(q_aug, k_aug, vt)
    return out_t.transpose(2, 0, 1).reshape(t, SB_WIDTH)


def _sb_sample_kernel(pt_ref, qbd_ref, bias_ref, kn_ref, vn_ref, *rest, pages_per_step, n_q):
    del pt_ref
    k_refs = rest[:pages_per_step]
    v_refs = rest[pages_per_step:2 * pages_per_step]
    o_ref, acc_ref, carry_ref = rest[2 * pages_per_step:]
    g = pl.program_id(1)
    qbd = qbd_ref[0]
    bias = bias_ref[...]
    page = k_refs[0].shape[3]
    kj = lax.broadcasted_iota(jnp.int32, (page, 2 * page), 0)
    ks = lax.broadcasted_iota(jnp.int32, (page, 2 * page), 1)
    later_or_all = jnp.where(jnp.logical_or(ks >= page, kj > ks), 1.0, 0.0)

    @pl.when(g == 0)
    def _():
        z = _dot_nt(qbd, kn_ref[0]) + bias
        t = lax.broadcasted_iota(jnp.int32, z.shape, 0) % n_q
        s = lax.broadcasted_iota(jnp.int32, z.shape, 1)
        vis = s < t
        sp = _softplus(z)
        sums = _dot_exact_rhs(jnp.where(vis, sp, 0.0), later_or_all)
        a = jnp.where(vis, jnp.exp(z - sp - sums[:, :page]), 0.0)
        acc_ref[...] = _dot(a, vn_ref[0])
        carry_ref[...] = sums[:, page:]

    acc = acc_ref[...]
    carry = carry_ref[...]
    flat = lambda ref: ref[0].reshape(SB_WIDTH, page)
    zs = [_dot(qbd, flat(k_ref)) + bias for k_ref in k_refs]
    sps = [_softplus(z) for z in zs]
    sums = [_dot_exact_rhs(sp, later_or_all) for sp in sps]
    for z, sp, sm, v_ref in zip(zs, sps, sums, v_refs):
        a = jnp.exp(z - sp - sm[:, :page] - carry)
        acc = acc + _dot_nt(a, flat(v_ref))
        carry = carry + sm[:, page:]
    acc_ref[...] = acc
    carry_ref[...] = carry

    @pl.when(g == pl.num_programs(1) - 1)
    def _():
        r = lax.broadcasted_iota(jnp.int32, acc.shape, 0)
        c = lax.broadcasted_iota(jnp.int32, acc.shape, 1)
        own = jnp.where(r // n_q == c // SB_HEAD_DIM, acc, 0.0)
        o_ref[0] = jnp.sum(own.reshape(SB_HEADS, n_q, SB_WIDTH), axis=0).astype(o_ref.dtype)


def _sb_sample(q, k_new, v_new, cache_k, cache_v, page_table, bias, pages_per_step):
    b, n_q, _ = q.shape
    n_pages = page_table.shape[1]
    page = cache_k.shape[1]
    ck = jnp.transpose(cache_k, (0, 2, 3, 1))
    cv = jnp.transpose(cache_v, (0, 2, 3, 1))
    q4 = (q * SB_HEAD_DIM ** -0.5).astype(BF16).reshape(b, n_q, SB_HEADS, SB_HEAD_DIM).transpose(0, 2, 1, 3)
    eye = jnp.eye(SB_HEADS, dtype=BF16)
    qbd = (q4[:, :, :, None, :] * eye[None, :, None, :, None]).reshape(b, SB_HEADS * n_q, SB_WIDTH)
    bias_col = jnp.repeat(bias.astype(F32), n_q).reshape(SB_HEADS * n_q, 1)
    pad_keys = lambda x: jnp.pad(x, ((0, 0), (0, page - n_q), (0, 0)))
    steps = n_pages // pages_per_step

    def page_map(u):
        return lambda s, g, pt: (pt[s, n_pages - 1 - (g * pages_per_step + u)], 0, 0, 0)

    page_specs = [pl.BlockSpec((1, SB_HEADS, SB_HEAD_DIM, page), page_map(u)) for u in range(pages_per_step)]
    per_seq = lambda shape: pl.BlockSpec((1,) + shape, lambda s, g, pt: (s, 0, 0))
    grid_spec = pltpu.PrefetchScalarGridSpec(
        num_scalar_prefetch=1,
        grid=(b, steps),
        in_specs=[per_seq((SB_HEADS * n_q, SB_WIDTH)),
                  pl.BlockSpec((SB_HEADS * n_q, 1), lambda s, g, pt: (0, 0)),
                  per_seq((page, SB_WIDTH)), per_seq((page, SB_WIDTH))] + page_specs + page_specs,
        out_specs=per_seq((n_q, SB_WIDTH)),
        scratch_shapes=[pltpu.VMEM((SB_HEADS * n_q, SB_WIDTH), F32),
                        pltpu.VMEM((SB_HEADS * n_q, page), F32)],
    )
    return pl.pallas_call(
        functools.partial(_sb_sample_kernel, pages_per_step=pages_per_step, n_q=n_q),
        grid_spec=grid_spec,
        out_shape=jax.ShapeDtypeStruct((b, n_q, SB_WIDTH), BF16),
        compiler_params=_params("parallel", "arbitrary"),
        name="sb_sample",
    )(page_table, qbd, bias_col, pad_keys(k_new), pad_keys(v_new),
      *([ck] * pages_per_step), *([cv] * pages_per_step))


def _gdn_kernel(hp_ref, qkv_ref, gate_ref, ab_ref, cbuf_ref, cw_ref, nw_ref, s0_ref,
                o_ref, s_out_ref, s_ref, carry_ref, *, chunk, n_sub):
    c = pl.program_id(1)

    @pl.when(c == 0)
    def _():
        for h in range(GDN_HEADS):
            s_ref[h] = s0_ref[0, h].T
        carry_ref[...] = cbuf_ref[0]

    n_rows = chunk * n_sub
    x = qkv_ref[...]
    ext = jnp.concatenate([carry_ref[...], x], axis=0)
    conv = ext[5:5 + n_rows] * cw_ref[0:1, :]
    for i in range(1, GDN_CONV):
        conv = conv + ext[5 + i:5 + i + n_rows] * cw_ref[i:i + 1, :]
    carry_ref[...] = ext[n_rows:n_rows + 8]
    act = _silu(conv)

    rows = lax.broadcasted_iota(jnp.int32, (chunk, chunk), 0)
    cols = lax.broadcasted_iota(jnp.int32, (chunk, chunk), 1)
    lower = cols <= rows
    strict = cols < rows
    lower01 = jnp.where(lower, 1.0, 0.0)
    ab = ab_ref[...]
    d = GDN_HEAD_DIM
    heads = range(GDN_HEADS)
    units = [(j, h) for j in range(n_sub) for h in heads]
    unit = lambda j, h: j * GDN_HEADS + h
    rows_of = lambda j: slice(j * chunk, (j + 1) * chunk)
    head_cols = lambda x, base, j, h: x[rows_of(j), base + h * d:base + (h + 1) * d]
    l2n = lambda x: x * lax.rsqrt(jnp.sum(x * x, axis=-1, keepdims=True) + NORM_EPS)
    qn = [l2n(head_cols(act, 0, j, h)) * (d ** -0.5) for j, h in units]
    kn = [l2n(head_cols(act, GDN_WIDTH, j, h)) for j, h in units]
    vh = [head_cols(act, 2 * GDN_WIDTH, j, h) for j, h in units]
    beta = [_sigmoid(ab[rows_of(j), GDN_HEADS + h:GDN_HEADS + h + 1]) for j, h in units]
    g = [-jnp.exp(hp_ref[0:1, h:h + 1]) * _softplus(ab[rows_of(j), h:h + 1] + hp_ref[1:2, h:h + 1])
         for j, h in units]
    gc = [_dot_exact_lhs(lower01, jnp.broadcast_to(x, (chunk, d))) for x in g]
    gi = [x[:, :chunk] if chunk <= d else jnp.broadcast_to(x[:, :1], (chunk, chunk)) for x in gc]
    decay = [jnp.exp(jnp.where(lower, x - x.T, -jnp.inf)) for x in gi]
    kb = [k_ * b_ for k_, b_ in zip(kn, beta)]
    tri = [jnp.where(strict, _dot_nt(kb_, k_) * dc, 0.0) for kb_, k_, dc in zip(kb, kn, decay)]
    attn = [jnp.where(lower, _dot_nt(q_, k_) * dc, 0.0) for q_, k_, dc in zip(qn, kn, decay)]
    t_inv = _inv_i_minus_many([-t for t in tri], chunk, chunk)
    e_gc = [jnp.exp(x) for x in gc]
    uw = [_dot(t_inv[i], jnp.concatenate([vh[i] * beta[i], kb[i] * e_gc[i]], axis=-1)) for i in range(len(units))]
    q_in = [q_ * e_ for q_, e_ in zip(qn, e_gc)]
    g_last = [x[chunk - 1:chunk, :] for x in gc]
    k_out = [k_ * jnp.exp(gl - x) for k_, gl, x in zip(kn, g_last, gc)]
    s = [s_ref[h] for h in heads]
    o = [None] * len(units)
    for j in range(n_sub):
        ids = [unit(j, h) for h in heads]
        v_new = [uw[i][:, :d] - _dot_nt(uw[i][:, d:], s[h]) for h, i in zip(heads, ids)]
        for h, i in zip(heads, ids):
            o[i] = _dot_nt(q_in[i], s[h]) + _dot(attn[i], v_new[h])
        s = [s[h] * jnp.exp(g_last[i]) + _dot_tn(v_new[h], k_out[i]) for h, i in zip(heads, ids)]
    for h in heads:
        s_ref[h] = s[h]
    for j, h in units:
        oi = o[unit(j, h)]
        on = oi * lax.rsqrt(jnp.mean(oi * oi, axis=-1, keepdims=True) + NORM_EPS) * nw_ref[...]
        gate = gate_ref[rows_of(j), h * d:(h + 1) * d]
        o_ref[rows_of(j), h * d:(h + 1) * d] = (on * _silu(gate)).astype(o_ref.dtype)

    @pl.when(c == pl.num_programs(1) - 1)
    def _():
        for h in range(GDN_HEADS):
            s_out_ref[0, h] = s_ref[h].T


def _gdn(y_all, row0, n_seq, seq_len, chunk, n_sub, conv_buf, state0, conv_w, a_log, dt_bias, norm_w):
    step_rows = chunk * n_sub
    n_chunks = seq_len // step_rows
    blk0 = row0 // step_rows
    row_map = lambda col: (lambda s, c: (blk0 + s * n_chunks + c, col))
    cbuf = jnp.pad(conv_buf, ((0, 0), (8 - (GDN_CONV - 1), 0), (0, 0)))
    cw = jnp.pad(conv_w, ((0, 8 - GDN_CONV), (0, 0)))
    head_params = jnp.zeros((8, 128), F32).at[0, :GDN_HEADS].set(a_log).at[1, :GDN_HEADS].set(dt_bias)
    state_spec = pl.BlockSpec((1, GDN_HEADS, GDN_HEAD_DIM, GDN_HEAD_DIM), lambda s, c: (s, 0, 0, 0))
    return pl.pallas_call(
        functools.partial(_gdn_kernel, chunk=chunk, n_sub=n_sub),
        grid=(n_seq, n_chunks),
        in_specs=[pl.BlockSpec((8, 128), lambda s, c: (0, 0)),
                  pl.BlockSpec((step_rows, 3 * GDN_WIDTH), row_map(1)),
                  pl.BlockSpec((step_rows, COL_BLOCK), row_map(6)),
                  pl.BlockSpec((step_rows, COL_BLOCK), row_map(7)),
                  pl.BlockSpec((1, 8, 3 * GDN_WIDTH), lambda s, c: (s, 0, 0)),
                  pl.BlockSpec((8, 3 * GDN_WIDTH), lambda s, c: (0, 0)),
                  pl.BlockSpec((1, GDN_HEAD_DIM), lambda s, c: (0, 0)),
                  state_spec],
        out_specs=[pl.BlockSpec((step_rows, GDN_WIDTH), lambda s, c: (s * n_chunks + c, 0)), state_spec],
        out_shape=[jax.ShapeDtypeStruct((n_seq * seq_len, GDN_WIDTH), BF16),
                   jax.ShapeDtypeStruct(state0.shape, F32)],
        scratch_shapes=[pltpu.VMEM((GDN_HEADS, GDN_HEAD_DIM, GDN_HEAD_DIM), F32),
                        pltpu.VMEM((8, 3 * GDN_WIDTH), F32)],
        compiler_params=_params("parallel", "arbitrary"),
        name="gdn",
    )(head_params, y_all, y_all, y_all, cbuf, cw, norm_w.reshape(1, GDN_HEAD_DIM), state0)


def _rwkv_kernel(r_ref, k_ref, v_ref, lw_ref, a_ref, g_ref, kk_ref, ka_ref, rk_ref, lg_ref, lb_ref,
                 s0_ref, o_ref, s_out_ref, s_ref, *, chunk, group):
    c = pl.program_id(1)
    hd = RWKV_HEAD_DIM
    width = group * hd
    size = group * chunk
    n_groups = D_MODEL // width

    @pl.when(c == 0)
    def _():
        for gi in range(n_groups):
            for hh in range(group):
                blocks = [s0_ref[0, gi * group + hh] if h2 == hh else jnp.zeros((hd, hd), F32)
                          for h2 in range(group)]
                s_ref[gi, hh * hd:(hh + 1) * hd, :] = jnp.concatenate(blocks, axis=-1)
    rows = lax.broadcasted_iota(jnp.int32, (size, size), 0)
    cols = lax.broadcasted_iota(jnp.int32, (size, size), 1)
    same = rows // chunk == cols // chunk
    lower = jnp.logical_and(same, cols <= rows)
    strict = jnp.logical_and(same, cols < rows)
    r2 = lax.broadcasted_iota(jnp.int32, (size, width), 0)
    c2 = lax.broadcasted_iota(jnp.int32, (size, width), 1)
    own = r2 // chunk == c2 // hd
    groups = range(D_MODEL // width)
    cols_of = lambda gi: slice(gi * width, (gi + 1) * width)
    tile = lambda x: jnp.concatenate([x] * group, axis=0)
    stack = lambda x: jnp.where(own, tile(x), 0.0)
    unstack = lambda x: sum(x[hh * chunk:(hh + 1) * chunk] for hh in range(group))

    tr = lax.broadcasted_iota(jnp.int32, (chunk, chunk), 0)
    tc = lax.broadcasted_iota(jnp.int32, (chunk, chunk), 1)
    lw_all = lw_ref[...]
    cw_all = _dot_exact_lhs(jnp.where(tc <= tr, 1.0, 0.0), lw_all)
    tot_all = jnp.sum(lw_all, axis=0, keepdims=True)
    a_all = a_ref[...]
    k_all = k_ref[...]
    v_all = v_ref[...]
    e_neg = jnp.exp(-cw_all)
    e_end = jnp.exp(tot_all - cw_all)
    neg_e_prev = -jnp.exp(cw_all - lw_all)
    r_cw = r_ref[...] * jnp.exp(cw_all)
    k2_all = k_all * (1.0 + (a_all - 1.0) * ka_ref[...])
    kk_raw = k_all * kk_ref[...]
    rk_all = r_ref[...] * k2_all * rk_ref[...]
    kt_all, k_end_all = k2_all * e_neg, k2_all * e_end

    kk = [stack(kk_raw[:, cols_of(gi)]) for gi in groups]
    kk = [x * lax.rsqrt(jnp.sum(x * x, axis=-1, keepdims=True) + NORM_EPS) for x in kk]
    b = [unstack(kk[gi]) * a_all[:, cols_of(gi)] for gi in groups]
    at = [kk[gi] * tile(neg_e_prev[:, cols_of(gi)]) for gi in groups]
    rt = [stack(r_cw[:, cols_of(gi)]) for gi in groups]
    v = [stack(v_all[:, cols_of(gi)]) for gi in groups]
    bt = [tile(b[gi] * e_neg[:, cols_of(gi)]) for gi in groups]
    kt = [tile(kt_all[:, cols_of(gi)]) for gi in groups]
    b_end = [stack(b[gi] * e_end[:, cols_of(gi)]) for gi in groups]
    k_end = [stack(k_end_all[:, cols_of(gi)]) for gi in groups]
    a_ab = [jnp.where(strict, _dot_nt(at[gi], bt[gi]), 0.0) for gi in groups]
    a_ak = [jnp.where(strict, _dot_nt(at[gi], kt[gi]), 0.0) for gi in groups]
    a_rb = [jnp.where(lower, _dot_nt(rt[gi], bt[gi]), 0.0) for gi in groups]
    a_rk = [jnp.where(lower, _dot_nt(rt[gi], kt[gi]), 0.0) for gi in groups]
    t_inv = _inv_i_minus_many(a_ab, size, chunk)
    s = [s_ref[gi] for gi in groups]
    rhs = [_dot_nt(at[gi], s[gi]) + _dot(a_ak[gi], v[gi]) for gi in groups]
    o_past = [_dot_nt(rt[gi], s[gi]) + _dot(a_rk[gi], v[gi]) for gi in groups]
    u = [_dot(t_inv[gi], rhs[gi]) for gi in groups]
    o = [o_past[gi] + _dot(a_rb[gi], u[gi]) for gi in groups]
    for gi in groups:
        s_ref[gi] = (s[gi] * jnp.exp(tot_all[:, cols_of(gi)]) + _dot_tn(u[gi], b_end[gi])
                     + _dot_tn(v[gi], k_end[gi]))
    for gi in groups:
        sl = cols_of(gi)
        mu = jnp.sum(o[gi], axis=-1, keepdims=True) * (1.0 / hd)
        oc = jnp.where(own, o[gi] - mu, 0.0)
        var = jnp.sum(oc * oc, axis=-1, keepdims=True) * (1.0 / hd)
        bonus = jnp.sum(stack(rk_all[:, sl]), axis=-1, keepdims=True) * v[gi]
        out = unstack(oc * lax.rsqrt(var + RWKV_GN_EPS)) * lg_ref[:, sl] + lb_ref[:, sl] + unstack(bonus)
        o_ref[:, sl] = (out * g_ref[:, sl]).astype(o_ref.dtype)

    @pl.when(c == pl.num_programs(1) - 1)
    def _():
        for gi in range(n_groups):
            for hh in range(group):
                s_out_ref[0, gi * group + hh] = s_ref[gi, hh * hd:(hh + 1) * hd, hh * hd:(hh + 1) * hd]


def _rwkv(r, k, v, lw, a, g, row0, n_seq, seq_len, chunk, state0, k_k, k_a, r_k, lnx_g, lnx_b, group=4):
    d = D_MODEL
    n_groups = RWKV_HEADS // group
    width = group * RWKV_HEAD_DIM
    n_chunks = seq_len // chunk
    blk0 = row0 // chunk
    tok = pl.BlockSpec((chunk, d), lambda s, c: (blk0 + s * n_chunks + c, 0))
    vec = pl.BlockSpec((1, d), lambda s, c: (0, 0))
    state_spec = pl.BlockSpec((1, RWKV_HEADS, RWKV_HEAD_DIM, RWKV_HEAD_DIM), lambda s, c: (s, 0, 0, 0))
    return pl.pallas_call(
        functools.partial(_rwkv_kernel, chunk=chunk, group=group),
        grid=(n_seq, n_chunks),
        in_specs=[tok] * 6 + [vec] * 5 + [state_spec],
        out_specs=[pl.BlockSpec((chunk, d), lambda s, c: (s * n_chunks + c, 0)), state_spec],
        out_shape=[jax.ShapeDtypeStruct((n_seq * seq_len, d), BF16),
                   jax.ShapeDtypeStruct(state0.shape, F32)],
        scratch_shapes=[pltpu.VMEM((n_groups, width, width), F32)],
        compiler_params=_params("parallel", "arbitrary"),
        name="rwkv_wkv",
    )(r, k, v, lw, a, g, k_k.reshape(1, d), k_a.reshape(1, d), r_k.reshape(1, d),
      lnx_g.reshape(1, d), lnx_b.reshape(1, d), state0)


def _forward(x_prompt, x_sample, cache_k, cache_v, page_table, state_gdn_conv, state_gdn,
             state_rwkv_shift, state_rwkv, p, *, tm, tm_proj, sb_tq, sb_tk, pages_per_step,
             chunk_prompt, gdn_chunk_sample, rwkv_chunk_sample, ffn_tf, tm_moe, moe_rows, moe_strip, gdn_sub):
    bp, t_p, d = x_prompt.shape
    bs, t_s, _ = x_sample.shape
    assert bp == 1
    n_p, n_s = bp * t_p, bs * t_s
    x = jnp.concatenate([x_prompt.reshape(n_p, d), x_sample.reshape(n_s, d)], axis=0)

    w_in = p['w_in0']
    cut = 3 * SB_WIDTH + 3 * GDN_WIDTH
    w_pad = jnp.concatenate(
        [w_in[:, :cut], w_in[:, cut + 2 * GDN_HEADS:], w_in[:, cut:cut + 2 * GDN_HEADS],
         jnp.zeros((d, COL_BLOCK - 2 * GDN_HEADS), F32)], axis=1).astype(BF16)
    y = _matmul(x, w_pad, tm, COL_BLOCK)
    q = y[:, :SB_WIDTH]
    k_rows = y[:, SB_WIDTH:2 * SB_WIDTH]
    v_rows = y[:, 2 * SB_WIDTH:3 * SB_WIDTH]
    gdn_rows = y[:, 3 * SB_WIDTH:3 * SB_WIDTH + 3 * GDN_WIDTH]

    o_sb_p = _sb_prompt(q[:n_p], k_rows[:n_p], v_rows[:n_p], p['sb_bias'], sb_tq, sb_tk)
    shape_s = (bs, t_s, SB_WIDTH)
    o_sb_s = _sb_sample(q[n_p:].reshape(shape_s), k_rows[n_p:].reshape(shape_s), v_rows[n_p:].reshape(shape_s),
                        cache_k, cache_v, page_table, p['sb_bias'], pages_per_step)
    o_sb = jnp.concatenate([o_sb_p, o_sb_s.reshape(n_s, SB_WIDTH)], axis=0)

    gdn_args = (p['gdn_conv_w'], p['gdn_a_log'], p['gdn_dt_bias'], p['gdn_norm_w'])
    o_gdn_p, gdn_state_p = _gdn(y, 0, bp, t_p, chunk_prompt, gdn_sub,
                                jnp.zeros((bp, GDN_CONV - 1, 3 * GDN_WIDTH), F32),
                                jnp.zeros((bp,) + state_gdn.shape[1:], F32), *gdn_args)
    o_gdn_s, gdn_state_s = _gdn(y, n_p, bs, t_s, gdn_chunk_sample, 1, state_gdn_conv, state_gdn, *gdn_args)
    o_gdn = jnp.concatenate([o_gdn_p, o_gdn_s], axis=0)
    conv_p = gdn_rows[:n_p].reshape(bp, t_p, -1)[:, t_p - (GDN_CONV - 1):]
    conv_s = gdn_rows[n_p:].reshape(bs, t_s, -1)[:, t_s - (GDN_CONV - 1):]

    w_out = p['w_out0'].astype(BF16)
    x = _proj_ln([o_sb, o_gdn], [w_out[:SB_WIDTH], w_out[SB_WIDTH:]], x,
                 p['ln_gamma'][0, 0], p['ln_beta'][0, 0], tm, "out_proj_ln")
    x = _ffn_ln(x, p['ffn_gate'].astype(BF16), p['ffn_up'].astype(BF16), p['ffn_down'].astype(BF16),
                p['ln_gamma'][0, 1], p['ln_beta'][0, 1], tm, ffn_tf)

    x_p = x[:n_p].reshape(bp, t_p, d)
    x_s = x[n_p:].reshape(bs, t_s, d)
    prev_p = jnp.concatenate([jnp.zeros((bp, 1, d), F32), x_p[:, :-1]], axis=1)
    prev_s = jnp.concatenate([state_rwkv_shift[:, None, :], x_s[:, :-1]], axis=1)
    x_prev = jnp.concatenate([prev_p.reshape(n_p, d), prev_s.reshape(n_s, d)], axis=0)
    bf = lambda name: p[name].astype(BF16)
    r, k, v, lw, a, g = _rwkv_proj(
        x, x_prev, p['rwkv_mix'], bf('rwkv_w_r'), bf('rwkv_w_k'), bf('rwkv_w_v'), bf('rwkv_w1'), bf('rwkv_w2'),
        bf('rwkv_a1'), bf('rwkv_a2'), bf('rwkv_g1'), bf('rwkv_g2'),
        p['rwkv_w0'].reshape(1, d), p['rwkv_a0'].reshape(1, d), tm_proj)
    rwkv_args = (p['rwkv_k_k'], p['rwkv_k_a'], p['rwkv_r_k'], p['rwkv_lnx_g'], p['rwkv_lnx_b'])
    o_p, rwkv_state_p = _rwkv(r, k, v, lw, a, g, 0, bp, t_p, chunk_prompt,
                              jnp.zeros((bp,) + state_rwkv.shape[1:], F32), *rwkv_args)
    o_s, rwkv_state_s = _rwkv(r, k, v, lw, a, g, n_p, bs, t_s, rwkv_chunk_sample, state_rwkv, *rwkv_args)
    x1 = _proj_ln([jnp.concatenate([o_p, o_s], axis=0)], [bf('rwkv_w_o')], x,
                  p['ln_gamma'][1, 0], p['ln_beta'][1, 0], tm, "rwkv_out_ln")
    w_router = jnp.pad(p['moe_router'], ((0, 0), (0, 128 - N_EXPERTS)))
    out = _moe_ln(x1, bf('moe_gate'), bf('moe_up'), bf('moe_down'), p['ln_gamma'][1, 1], p['ln_beta'][1, 1],
                  w_router, tm_moe, ffn_tf, moe_rows, moe_strip)

    heads = lambda rows, b_, t_: rows.reshape(b_, t_, SB_HEADS, SB_HEAD_DIM)
    return (out[:n_p].reshape(bp, t_p, d), out[n_p:].reshape(bs, t_s, d),
            heads(k_rows[:n_p], bp, t_p), heads(v_rows[:n_p], bp, t_p), conv_p, gdn_state_p,
            x_p[:, -1], rwkv_state_p,
            heads(k_rows[n_p:], bs, t_s), heads(v_rows[n_p:], bs, t_s), conv_s, gdn_state_s,
            x_s[:, -1], rwkv_state_s)


def kernel(x_prompt, x_sample, cache_k, cache_v, page_table, state_gdn_conv, state_gdn, state_rwkv_shift, state_rwkv, w_in0, sb_bias, gdn_conv_w, gdn_a_log, gdn_dt_bias, gdn_norm_w, w_out0, ffn_gate, ffn_up, ffn_down, rwkv_mix, rwkv_w_r, rwkv_w_k, rwkv_w_v, rwkv_w0, rwkv_w1, rwkv_w2, rwkv_a0, rwkv_a1, rwkv_a2, rwkv_g1, rwkv_g2, rwkv_k_k, rwkv_k_a, rwkv_r_k, rwkv_lnx_g, rwkv_lnx_b, rwkv_w_o, moe_router, moe_gate, moe_up, moe_down, ln_gamma, ln_beta):
    p = dict(w_in0=w_in0, sb_bias=sb_bias, gdn_conv_w=gdn_conv_w, gdn_a_log=gdn_a_log, gdn_dt_bias=gdn_dt_bias,
             gdn_norm_w=gdn_norm_w, w_out0=w_out0, ffn_gate=ffn_gate, ffn_up=ffn_up, ffn_down=ffn_down,
             rwkv_mix=rwkv_mix, rwkv_w_r=rwkv_w_r, rwkv_w_k=rwkv_w_k, rwkv_w_v=rwkv_w_v,
             rwkv_w0=rwkv_w0, rwkv_w1=rwkv_w1, rwkv_w2=rwkv_w2, rwkv_a0=rwkv_a0, rwkv_a1=rwkv_a1,
             rwkv_a2=rwkv_a2, rwkv_g1=rwkv_g1, rwkv_g2=rwkv_g2, rwkv_k_k=rwkv_k_k, rwkv_k_a=rwkv_k_a,
             rwkv_r_k=rwkv_r_k, rwkv_lnx_g=rwkv_lnx_g, rwkv_lnx_b=rwkv_lnx_b, rwkv_w_o=rwkv_w_o,
             moe_router=moe_router, moe_gate=moe_gate, moe_up=moe_up, moe_down=moe_down,
             ln_gamma=ln_gamma, ln_beta=ln_beta)
    return _forward(x_prompt, x_sample, cache_k, cache_v, page_table, state_gdn_conv, state_gdn,
                    state_rwkv_shift, state_rwkv, p, tm=640, tm_proj=320, sb_tq=512, sb_tk=256,
                    pages_per_step=16, chunk_prompt=64, gdn_chunk_sample=8, rwkv_chunk_sample=8, ffn_tf=1408,
                    tm_moe=1280, moe_rows=352, moe_strip=256, gdn_sub=4)
```

```python
import functools
import math

import jax
import jax.numpy as jnp
import numpy as np
from jax import lax
from jax.experimental import pallas as pl
from jax.experimental.pallas import tpu as pltpu

F32 = jnp.float32
BF16 = jnp.bfloat16

D_MODEL = 1024
SB_HEADS = 8
SB_HEAD_DIM = 64
SB_WIDTH = SB_HEADS * SB_HEAD_DIM
GDN_HEADS = 4
GDN_HEAD_DIM = 128
GDN_WIDTH = GDN_HEADS * GDN_HEAD_DIM
GDN_CONV = 4
RWKV_HEAD_DIM = 64
RWKV_HEADS = D_MODEL // RWKV_HEAD_DIM
RWKV_GN_EPS = 64e-5
N_EXPERTS = 8
DEPTH = 2
DEEPNORM_ALPHA = (2 * DEPTH) ** 0.25
LN_EPS = 1e-5
NORM_EPS = 1e-6
LOG2E = math.log2(math.e)

IN0_COLS = 4096
COL_BLOCK = 512
VMEM_LIMIT_BYTES = 56 * 1024 * 1024

_NT = (((1,), (1,)), ((), ()))
_TN = (((0,), (0,)), ((), ()))


def _params(*sem):
    return pltpu.CompilerParams(dimension_semantics=sem, vmem_limit_bytes=VMEM_LIMIT_BYTES)


def _dot(a, b):
    return jnp.dot(a.astype(BF16), b.astype(BF16), preferred_element_type=F32)


def _dot_nt(a, b):
    return lax.dot_general(a.astype(BF16), b.astype(BF16), _NT, preferred_element_type=F32)


def _dot_tn(a, b):
    return lax.dot_general(a.astype(BF16), b.astype(BF16), _TN, preferred_element_type=F32)


def _dot_exact_lhs(a01, x):
    a = a01.astype(BF16)
    x1 = x.astype(BF16)
    r1 = x - x1.astype(F32)
    x2 = r1.astype(BF16)
    x3 = (r1 - x2.astype(F32)).astype(BF16)
    out = jnp.dot(a, x1, preferred_element_type=F32)
    out = out + jnp.dot(a, x2, preferred_element_type=F32)
    return out + jnp.dot(a, x3, preferred_element_type=F32)


def _dot_exact_rhs(x, b01):
    b = b01.astype(BF16)
    x1 = x.astype(BF16)
    r1 = x - x1.astype(F32)
    x2 = r1.astype(BF16)
    x3 = (r1 - x2.astype(F32)).astype(BF16)
    out = jnp.dot(x1, b, preferred_element_type=F32)
    out = out + jnp.dot(x2, b, preferred_element_type=F32)
    return out + jnp.dot(x3, b, preferred_element_type=F32)


def _softplus(z):
    return jnp.maximum(z, 0.0) + jnp.log1p(jnp.exp(-jnp.abs(z)))


def _sigmoid(z):
    return 1.0 / (1.0 + jnp.exp(-z))


def _silu(z):
    return z * _sigmoid(z)


def _layer_norm(x, g, b):
    mu = jnp.mean(x, axis=-1, keepdims=True)
    xc = x - mu
    var = jnp.mean(xc * xc, axis=-1, keepdims=True)
    return xc * lax.rsqrt(var + LN_EPS) * g + b


def _inv_i_minus(n, size, block):
    return _inv_i_minus_many([n], size, block)[0]


def _inv_i_minus_many(ns, size, block):
    rows = lax.broadcasted_iota(jnp.int32, (size, size), 0)
    cols = lax.broadcasted_iota(jnp.int32, (size, size), 1)
    eye = jnp.where(rows == cols, 1.0, 0.0)
    ps = [eye + n for n in ns]
    ys = list(ns)
    for _ in range(max(0, int(math.ceil(math.log2(block))) - 1)):
        ys = [_dot(y, y) for y in ys]
        ps = [p + _dot(p, y) for p, y in zip(ps, ys)]
    return ps


def _matmul_kernel(x_ref, w_ref, o_ref, *, tn):
    xb = x_ref[...].astype(BF16)
    for j in range(w_ref.shape[1] // tn):
        o_ref[:, j * tn:(j + 1) * tn] = jnp.dot(xb, w_ref[:, j * tn:(j + 1) * tn], preferred_element_type=F32)


def _matmul(x, w, tm, tn):
    n, k = x.shape
    m = w.shape[1]
    return pl.pallas_call(
        functools.partial(_matmul_kernel, tn=tn),
        grid=(n // tm,),
        in_specs=[pl.BlockSpec((tm, k), lambda i: (i, 0)),
                  pl.BlockSpec((k, m), lambda i: (0, 0))],
        out_specs=pl.BlockSpec((tm, m), lambda i: (i, 0)),
        out_shape=jax.ShapeDtypeStruct((n, m), F32),
        compiler_params=_params("parallel"),
        name="in_proj",
    )(x, w)


def _proj_ln_kernel(*refs, n_in):
    a_refs = refs[:n_in]
    w_refs = refs[n_in:2 * n_in]
    x_ref, g_ref, b_ref, o_ref = refs[2 * n_in:]
    h = jnp.dot(a_refs[0][...], w_refs[0][...], preferred_element_type=F32)
    for a_ref, w_ref in zip(a_refs[1:], w_refs[1:]):
        h = h + jnp.dot(a_ref[...], w_ref[...], preferred_element_type=F32)
    o_ref[...] = _layer_norm(DEEPNORM_ALPHA * x_ref[...] + h, g_ref[...], b_ref[...])


def _proj_ln(acts, weights, x, gamma, beta, tm, name):
    n, d = x.shape
    n_in = len(acts)
    in_specs = [pl.BlockSpec((tm, a.shape[1]), lambda i: (i, 0)) for a in acts]
    in_specs += [pl.BlockSpec(w.shape, lambda i: (0, 0)) for w in weights]
    in_specs += [pl.BlockSpec((tm, d), lambda i: (i, 0)),
                 pl.BlockSpec((1, d), lambda i: (0, 0)),
                 pl.BlockSpec((1, d), lambda i: (0, 0))]
    return pl.pallas_call(
        functools.partial(_proj_ln_kernel, n_in=n_in),
        grid=(n // tm,),
        in_specs=in_specs,
        out_specs=pl.BlockSpec((tm, d), lambda i: (i, 0)),
        out_shape=jax.ShapeDtypeStruct((n, d), F32),
        compiler_params=_params("parallel"),
        name=name,
    )(*acts, *weights, x, gamma.reshape(1, d), beta.reshape(1, d))


def _top2_gates(logits):
    lane = lax.broadcasted_iota(jnp.int32, logits.shape, 1).astype(F32)
    big = float(logits.shape[1])
    lg = jnp.where(lane < N_EXPERTS, logits, -jnp.inf)
    m1 = jnp.max(lg, axis=-1, keepdims=True)
    i1 = jnp.min(jnp.where(lg == m1, lane, big), axis=-1, keepdims=True)
    lg2 = jnp.where(lane == i1, -jnp.inf, lg)
    m2 = jnp.max(lg2, axis=-1, keepdims=True)
    i2 = jnp.min(jnp.where(lg2 == m2, lane, big), axis=-1, keepdims=True)
    e2 = jnp.exp(m2 - m1)
    den = 1.0 + e2
    gates = jnp.where(lane == i1, 1.0 / den, 0.0) + jnp.where(lane == i2, e2 / den, 0.0)
    chosen = jnp.where(jnp.logical_or(lane == i1, lane == i2), 1.0, 0.0)
    return gates, chosen


def _ffn_ln_kernel(x_ref, wg_ref, wu_ref, wd_ref, g_ref, b_ref, o_ref, acc_ref):
    f = pl.program_id(1)

    @pl.when(f == 0)
    def _():
        acc_ref[...] = jnp.zeros_like(acc_ref)

    xb = x_ref[...].astype(BF16)
    hg = jnp.dot(xb, wg_ref[...], preferred_element_type=F32)
    hu = jnp.dot(xb, wu_ref[...], preferred_element_type=F32)
    acc_ref[...] += jnp.dot((_silu(hg) * hu).astype(BF16), wd_ref[...], preferred_element_type=F32)

    @pl.when(f == pl.num_programs(1) - 1)
    def _():
        o_ref[...] = _layer_norm(DEEPNORM_ALPHA * x_ref[...] + acc_ref[...], g_ref[...], b_ref[...])


def _ffn_ln(x, w_gate, w_up, w_down, gamma, beta, tm, tf):
    n, d = x.shape
    d_ff = w_gate.shape[1]
    return pl.pallas_call(
        _ffn_ln_kernel,
        grid=(n // tm, d_ff // tf),
        in_specs=[pl.BlockSpec((tm, d), lambda i, f: (i, 0)),
                  pl.BlockSpec((d, tf), lambda i, f: (0, f)),
                  pl.BlockSpec((d, tf), lambda i, f: (0, f)),
                  pl.BlockSpec((tf, d), lambda i, f: (f, 0)),
                  pl.BlockSpec((1, d), lambda i, f: (0, 0)),
                  pl.BlockSpec((1, d), lambda i, f: (0, 0))],
        out_specs=pl.BlockSpec((tm, d), lambda i, f: (i, 0)),
        out_shape=jax.ShapeDtypeStruct((n, d), F32),
        scratch_shapes=[pltpu.VMEM((tm, d), F32)],
        compiler_params=_params("parallel", "arbitrary"),
        name="ffn_ln",
    )(x, w_gate, w_up, w_down, gamma.reshape(1, d), beta.reshape(1, d))


def _moe_kernel(x_ref, wg_ref, wu_ref, wd_ref, g_ref, b_ref, wr_ref, o_ref,
                xb_ref, gate_ref, key_ref, keyt_ref, xe_ref, ye_ref, *, rows, strip):
    tm, d = x_ref.shape
    e = pl.program_id(1)
    f = pl.program_id(2)
    n_f = pl.num_programs(2)
    n_lane = gate_ref.shape[1]

    @pl.when(jnp.logical_and(e == 0, f == 0))
    def _():
        x = x_ref[...]
        xb_ref[...] = x.astype(BF16)
        logits = jnp.dot(x, wr_ref[...], preferred_element_type=F32, precision=lax.Precision.HIGHEST)
        gates, chosen = _top2_gates(logits)
        gate_ref[...] = gates
        tr = lax.broadcasted_iota(jnp.int32, (tm, tm), 0)
        tc = lax.broadcasted_iota(jnp.int32, (tm, tm), 1)
        before = jnp.where(tc < tr, 1.0, 0.0).astype(BF16)
        rank = jnp.dot(before, chosen.astype(BF16), preferred_element_type=F32)
        key = jnp.where(chosen > 0.0, rank, -1.0)
        key_ref[...] = key
        keyt_ref[...] = key.T
        o_ref[...] = jnp.zeros_like(o_ref)

    lane = lax.broadcasted_iota(jnp.int32, (tm, n_lane), 1)
    key_col = jnp.sum(jnp.where(lane == e, key_ref[...], 0.0), axis=-1, keepdims=True)
    count = jnp.sum(jnp.where(key_col >= 0.0, 1.0, 0.0)).astype(jnp.int32)
    n_chunks = (count + rows - 1) // rows

    @pl.when(f == 0)
    def _():
        key_row = keyt_ref[pl.ds(e, 1), :]

        def gather(c, carry):
            slot = (c * rows + lax.broadcasted_iota(jnp.int32, (rows, tm), 0)).astype(F32)
            sel = jnp.where(key_row == slot, 1.0, 0.0).astype(BF16)
            start = pl.multiple_of(c * rows, rows)
            xe_ref[pl.ds(start, rows), :] = jnp.dot(sel, xb_ref[...], preferred_element_type=F32).astype(BF16)
            return carry

        lax.fori_loop(0, n_chunks, gather, 0)

    def expert(c, carry):
        start = pl.multiple_of(c * rows, rows)
        xe = xe_ref[pl.ds(start, rows), :]
        hg = jnp.dot(xe, wg_ref[0], preferred_element_type=F32)
        hu = jnp.dot(xe, wu_ref[0], preferred_element_type=F32)
        y = jnp.dot((_silu(hg) * hu).astype(BF16), wd_ref[0], preferred_element_type=F32)

        @pl.when(f == 0)
        def _():
            ye_ref[pl.ds(start, rows), :] = y

        @pl.when(f > 0)
        def _():
            ye_ref[pl.ds(start, rows), :] += y

        return carry

    lax.fori_loop(0, n_chunks, expert, 0)

    @pl.when(f == n_f - 1)
    def _():
        gate_col = jnp.sum(jnp.where(lane == e, gate_ref[...], 0.0), axis=-1, keepdims=True)

        def scatter(c, carry):
            start = pl.multiple_of(c * rows, rows)
            y = ye_ref[pl.ds(start, rows), :].astype(BF16)
            for s in range(tm // strip):
                tok = slice(s * strip, (s + 1) * strip)
                slot = (c * rows + lax.broadcasted_iota(jnp.int32, (strip, rows), 1)).astype(F32)
                sel = jnp.where(key_col[tok] == slot, 1.0, 0.0).astype(BF16)
                o_ref[tok, :] += gate_col[tok] * jnp.dot(sel, y, preferred_element_type=F32)
            return carry

        lax.fori_loop(0, n_chunks, scatter, 0)

    @pl.when(jnp.logical_and(e == pl.num_programs(1) - 1, f == n_f - 1))
    def _():
        for s in range(tm // strip):
            tok = slice(s * strip, (s + 1) * strip)
            o_ref[tok, :] = _layer_norm(DEEPNORM_ALPHA * x_ref[tok, :] + o_ref[tok, :], g_ref[...], b_ref[...])


def _moe_ln(x, w_gate, w_up, w_down, gamma, beta, w_router, tm, tf, rows, strip):
    n, d = x.shape
    n_e, _, d_ff = w_gate.shape
    assert tm % strip == 0
    cap = -(-tm // rows) * rows
    once = dict(pipeline_mode=pl.Buffered(1))
    return pl.pallas_call(
        functools.partial(_moe_kernel, rows=rows, strip=strip),
        grid=(n // tm, n_e, d_ff // tf),
        in_specs=[pl.BlockSpec((tm, d), lambda i, e, f: (i, 0), **once),
                  pl.BlockSpec((1, d, tf), lambda i, e, f: (e, 0, f)),
                  pl.BlockSpec((1, d, tf), lambda i, e, f: (e, 0, f)),
                  pl.BlockSpec((1, tf, d), lambda i, e, f: (e, f, 0)),
                  pl.BlockSpec((1, d), lambda i, e, f: (0, 0)),
                  pl.BlockSpec((1, d), lambda i, e, f: (0, 0)),
                  pl.BlockSpec(w_router.shape, lambda i, e, f: (0, 0))],
        out_specs=pl.BlockSpec((tm, d), lambda i, e, f: (i, 0)),
        out_shape=jax.ShapeDtypeStruct((n, d), F32),
        scratch_shapes=[pltpu.VMEM((tm, d), BF16),
                        pltpu.VMEM((tm, w_router.shape[1]), F32),
                        pltpu.VMEM((tm, w_router.shape[1]), F32),
                        pltpu.VMEM((w_router.shape[1], tm), F32),
                        pltpu.VMEM((cap, d), BF16),
                        pltpu.VMEM((cap, d), F32)],
        compiler_params=_params("parallel", "arbitrary", "arbitrary"),
        name="moe_ln",
    )(x, w_gate, w_up, w_down, gamma.reshape(1, d), beta.reshape(1, d), w_router)


def _rwkv_proj_kernel(x_ref, xp_ref, mix_ref, wr_ref, wk_ref, wv_ref, w1_ref, w2_ref, a1_ref, a2_ref,
                      g1_ref, g2_ref, w0_ref, a0_ref, r_ref, k_ref, v_ref, lw_ref, a_ref, g_ref):
    x = x_ref[...]
    xx = xp_ref[...] - x
    mixed = lambda i: (x + xx * mix_ref[i:i + 1, :]).astype(BF16)
    r_ref[...] = jnp.dot(mixed(0), wr_ref[...], preferred_element_type=F32)
    w_lora = _dot(jnp.tanh(jnp.dot(mixed(1), w1_ref[...], preferred_element_type=F32)), w2_ref[...])
    lw_ref[...] = -jnp.exp(-_softplus(-(w0_ref[...] + w_lora)) - 0.5)
    k_ref[...] = jnp.dot(mixed(2), wk_ref[...], preferred_element_type=F32)
    v_ref[...] = jnp.dot(mixed(3), wv_ref[...], preferred_element_type=F32)
    a_lora = _dot(jnp.dot(mixed(4), a1_ref[...], preferred_element_type=F32), a2_ref[...])
    a_ref[...] = _sigmoid(a0_ref[...] + a_lora)
    g_ref[...] = _dot(_sigmoid(jnp.dot(mixed(5), g1_ref[...], preferred_element_type=F32)), g2_ref[...])


def _rwkv_proj(x, x_prev, mix, w_r, w_k, w_v, w1, w2, a1, a2, g1, g2, w0, a0, tm):
    n, d = x.shape
    tok = pl.BlockSpec((tm, d), lambda i: (i, 0))
    full = lambda a: pl.BlockSpec(a.shape, lambda i: (0, 0))
    consts = [mix, w_r, w_k, w_v, w1, w2, a1, a2, g1, g2, w0, a0]
    return pl.pallas_call(
        _rwkv_proj_kernel,
        grid=(n // tm,),
        in_specs=[tok, tok] + [full(c) for c in consts],
        out_specs=[tok] * 6,
        out_shape=[jax.ShapeDtypeStruct((n, d), F32)] * 6,
        compiler_params=_params("parallel"),
        name="rwkv_proj",
    )(x, x_prev, *consts)


def _suffix_sum_rows8(x):
    row = lax.broadcasted_iota(jnp.int32, x.shape, 0)
    for sh in (1, 2, 4):
        shifted = pltpu.roll(x, 8 - sh, axis=0)
        x = x + jnp.where(row < 8 - sh, shifted, 0.0)
    return x


def _sb_block(z_ref, r_ref, a_ref, carry, visible):
    tk, tq = z_ref.shape
    nsub = tk // 8
    run = jnp.zeros((8, tq), F32)
    for s in reversed(range(nsub)):
        rows = slice(s * 8, (s + 1) * 8)
        z = z_ref[rows, :]
        neg_abs = pltpu.bitcast(pltpu.bitcast(z, jnp.uint32) | jnp.uint32(0x80000000), F32)
        sp = jnp.maximum(z, 0.0) + jnp.log2(1.0 + jnp.exp2(neg_abs))
        vis = visible(s * 8, 8)
        if vis is None:
            run = run + sp
            r_ref[rows, :] = z - run
        else:
            r_ref[rows, :] = (z - sp) - run
            run = run + jnp.where(vis, sp, 0.0)
    incl = _suffix_sum_rows8(run)
    offset = incl - run + carry
    offset2 = jnp.concatenate([offset, offset], axis=0)
    for s in range(nsub // 2):
        rows = slice(s * 16, (s + 1) * 16)
        a = jnp.exp2(r_ref[rows, :] - offset2)
        vis = visible(s * 16, 16)
        if vis is not None:
            a = jnp.where(vis, a, 0.0)
        a_ref[rows, :] = a.astype(BF16)
    return carry + incl[0:1]


def _sb_prompt_kernel(q_ref, k_ref, vt_ref, o_ref, z0_ref, z1_ref, a0_ref, a1_ref, *, tq, tk):
    assert tq == 2 * tk
    i = pl.program_id(1)
    q = q_ref[0]
    nsub = tk // 8

    def scores(j):
        start = pl.multiple_of(jnp.maximum(j, 0) * tk, tk)
        return lax.dot_general(k_ref[0, pl.ds(start, tk), :], q, _NT, preferred_element_type=F32)

    def weighted_v(j, a_ref):
        start = pl.multiple_of(j * tk, tk)
        return jnp.dot(vt_ref[0, :, pl.ds(start, tk)], a_ref[...], preferred_element_type=F32)

    def weights(z_ref, a_ref, j, carry, masked):
        def visible(first_row, n_rows):
            if not masked:
                return None
            row = first_row + lax.broadcasted_iota(jnp.int32, (n_rows, tq), 0)
            col = lax.broadcasted_iota(jnp.int32, (n_rows, tq), 1)
            k_pos = j * tk + (row % 8) * nsub + row // 8
            return k_pos < i * tq + col
        return _sb_block(z_ref, z_ref, a_ref, carry, visible)

    newest = 2 * i + 1
    acc = jnp.zeros((SB_HEAD_DIM, tq), F32)
    carry = jnp.zeros((1, tq), F32)
    z0_ref[...] = scores(newest)
    z1_ref[...] = scores(newest - 1)
    carry = weights(z0_ref, a0_ref, newest, carry, True)
    z0_ref[...] = scores(newest - 2)
    acc = acc + weighted_v(newest, a0_ref)
    carry = weights(z1_ref, a1_ref, newest - 1, carry, True)

    def body(t, state):
        acc, carry = state
        blk = newest - 2 - 2 * t
        z1_ref[...] = scores(blk - 1)
        acc = acc + weighted_v(blk + 1, a1_ref)
        carry = weights(z0_ref, a0_ref, blk, carry, False)
        z0_ref[...] = scores(blk - 2)
        acc = acc + weighted_v(blk, a0_ref)
        carry = weights(z1_ref, a1_ref, blk - 1, carry, False)
        return acc, carry

    acc, carry = lax.fori_loop(0, i, body, (acc, carry))
    acc = acc + weighted_v(0, a1_ref)
    o_ref[0] = acc.astype(o_ref.dtype)


def _bf16_pieces(x, n):
    out = []
    for _ in range(n):
        piece = x.astype(BF16)
        out.append(piece)
        x = x - piece.astype(F32)
    return out


def _sb_prompt(q, k, v, bias, tq, tk):
    t = q.shape[0]
    nblk, nsub = t // tk, tk // 8
    pad = 128 - SB_HEAD_DIM
    qh = (q * (SB_HEAD_DIM ** -0.5 * LOG2E)).astype(BF16).reshape(t, SB_HEADS, SB_HEAD_DIM).transpose(1, 0, 2)
    bias_cols = jnp.stack(_bf16_pieces(bias.astype(F32) * LOG2E, 3), axis=-1)
    q_extra = jnp.pad(bias_cols, ((0, 0), (0, pad - 3)))[:, None, :]
    q_aug = jnp.concatenate([qh, jnp.broadcast_to(q_extra, (SB_HEADS, t, pad))], axis=-1)
    k5 = k.astype(BF16).reshape(nblk, 8, nsub, SB_HEADS, SB_HEAD_DIM)
    kh = k5.transpose(3, 0, 2, 1, 4).reshape(SB_HEADS, t, SB_HEAD_DIM)
    k_extra = jnp.pad(jnp.ones((3,), BF16), (0, pad - 3))
    k_aug = jnp.concatenate([kh, jnp.broadcast_to(k_extra, (SB_HEADS, t, pad))], axis=-1)
    v5 = v.astype(BF16).reshape(nblk, 8, nsub, SB_HEADS, SB_HEAD_DIM)
    vt = v5.transpose(3, 4, 0, 2, 1).reshape(SB_HEADS, SB_HEAD_DIM, t)
    out_t = pl.pallas_call(
        functools.partial(_sb_prompt_kernel, tq=tq, tk=tk),
        grid=(SB_HEADS, t // tq),
        in_specs=[pl.BlockSpec((1, tq, 128), lambda h, i: (h, i, 0)),
                  pl.BlockSpec((1, t, 128), lambda h, i: (h, 0, 0)),
                  pl.BlockSpec((1, SB_HEAD_DIM, t), lambda h, i: (h, 0, 0))],
        out_specs=pl.BlockSpec((1, SB_HEAD_DIM, tq), lambda h, i: (h, 0, i)),
        out_shape=jax.ShapeDtypeStruct((SB_HEADS, SB_HEAD_DIM, t), BF16),
        scratch_shapes=[pltpu.VMEM((tk, tq), F32), pltpu.VMEM((tk, tq), F32),
                        pltpu.VMEM((tk, tq), BF16), pltpu.VMEM((tk, tq), BF16)],
        compiler_params=_params("parallel", "parallel"),
        name="sb_prompt",
    )(q_aug, k_aug, vt)
    return out_t.transpose(2, 0, 1).reshape(t, SB_WIDTH)


def _sb_sample_kernel(pt_ref, qbd_ref, bias_ref, kn_ref, vn_ref, *rest, pages_per_step, n_q):
    del pt_ref
    k_refs = rest[:pages_per_step]
    v_refs = rest[pages_per_step:2 * pages_per_step]
    o_ref, acc_ref, carry_ref = rest[2 * pages_per_step:]
    g = pl.program_id(1)
    qbd = qbd_ref[0]
    bias = bias_ref[...]
    page = k_refs[0].shape[3]
    kj = lax.broadcasted_iota(jnp.int32, (page, 2 * page), 0)
    ks = lax.broadcasted_iota(jnp.int32, (page, 2 * page), 1)
    later_or_all = jnp.where(jnp.logical_or(ks >= page, kj > ks), 1.0, 0.0)

    @pl.when(g == 0)
    def _():
        z = _dot_nt(qbd, kn_ref[0]) + bias
        t = lax.broadcasted_iota(jnp.int32, z.shape, 0) % n_q
        s = lax.broadcasted_iota(jnp.int32, z.shape, 1)
        vis = s < t
        sp = _softplus(z)
        sums = _dot_exact_rhs(jnp.where(vis, sp, 0.0), later_or_all)
        a = jnp.where(vis, jnp.exp(z - sp - sums[:, :page]), 0.0)
        acc_ref[...] = _dot(a, vn_ref[0])
        carry_ref[...] = sums[:, page:]

    acc = acc_ref[...]
    carry = carry_ref[...]
    flat = lambda ref: ref[0].reshape(SB_WIDTH, page)
    zs = [_dot(qbd, flat(k_ref)) + bias for k_ref in k_refs]
    sps = [_softplus(z) for z in zs]
    sums = [_dot_exact_rhs(sp, later_or_all) for sp in sps]
    for z, sp, sm, v_ref in zip(zs, sps, sums, v_refs):
        a = jnp.exp(z - sp - sm[:, :page] - carry)
        acc = acc + _dot_nt(a, flat(v_ref))
        carry = carry + sm[:, page:]
    acc_ref[...] = acc
    carry_ref[...] = carry

    @pl.when(g == pl.num_programs(1) - 1)
    def _():
        r = lax.broadcasted_iota(jnp.int32, acc.shape, 0)
        c = lax.broadcasted_iota(jnp.int32, acc.shape, 1)
        own = jnp.where(r // n_q == c // SB_HEAD_DIM, acc, 0.0)
        o_ref[0] = jnp.sum(own.reshape(SB_HEADS, n_q, SB_WIDTH), axis=0).astype(o_ref.dtype)


def _sb_sample(q, k_new, v_new, cache_k, cache_v, page_table, bias, pages_per_step):
    b, n_q, _ = q.shape
    n_pages = page_table.shape[1]
    page = cache_k.shape[1]
    ck = jnp.transpose(cache_k, (0, 2, 3, 1))
    cv = jnp.transpose(cache_v, (0, 2, 3, 1))
    q4 = (q * SB_HEAD_DIM ** -0.5).astype(BF16).reshape(b, n_q, SB_HEADS, SB_HEAD_DIM).transpose(0, 2, 1, 3)
    eye = jnp.eye(SB_HEADS, dtype=BF16)
    qbd = (q4[:, :, :, None, :] * eye[None, :, None, :, None]).reshape(b, SB_HEADS * n_q, SB_WIDTH)
    bias_col = jnp.repeat(bias.astype(F32), n_q).reshape(SB_HEADS * n_q, 1)
    pad_keys = lambda x: jnp.pad(x, ((0, 0), (0, page - n_q), (0, 0)))
    steps = n_pages // pages_per_step

    def page_map(u):
        return lambda s, g, pt: (pt[s, n_pages - 1 - (g * pages_per_step + u)], 0, 0, 0)

    page_specs = [pl.BlockSpec((1, SB_HEADS, SB_HEAD_DIM, page), page_map(u)) for u in range(pages_per_step)]
    per_seq = lambda shape: pl.BlockSpec((1,) + shape, lambda s, g, pt: (s, 0, 0))
    grid_spec = pltpu.PrefetchScalarGridSpec(
        num_scalar_prefetch=1,
        grid=(b, steps),
        in_specs=[per_seq((SB_HEADS * n_q, SB_WIDTH)),
                  pl.BlockSpec((SB_HEADS * n_q, 1), lambda s, g, pt: (0, 0)),
                  per_seq((page, SB_WIDTH)), per_seq((page, SB_WIDTH))] + page_specs + page_specs,
        out_specs=per_seq((n_q, SB_WIDTH)),
        scratch_shapes=[pltpu.VMEM((SB_HEADS * n_q, SB_WIDTH), F32),
                        pltpu.VMEM((SB_HEADS * n_q, page), F32)],
    )
    return pl.pallas_call(
        functools.partial(_sb_sample_kernel, pages_per_step=pages_per_step, n_q=n_q),
        grid_spec=grid_spec,
        out_shape=jax.ShapeDtypeStruct((b, n_q, SB_WIDTH), BF16),
        compiler_params=_params("parallel", "arbitrary"),
        name="sb_sample",
    )(page_table, qbd, bias_col, pad_keys(k_new), pad_keys(v_new),
      *([ck] * pages_per_step), *([cv] * pages_per_step))


def _gdn_kernel(hp_ref, qkv_ref, gate_ref, ab_ref, cbuf_ref, cw_ref, nw_ref, s0_ref,
                o_ref, s_out_ref, s_ref, carry_ref, *, chunk, n_sub):
    c = pl.program_id(1)

    @pl.when(c == 0)
    def _():
        for h in range(GDN_HEADS):
            s_ref[h] = s0_ref[0, h].T
        carry_ref[...] = cbuf_ref[0]

    n_rows = chunk * n_sub
    x = qkv_ref[...]
    ext = jnp.concatenate([carry_ref[...], x], axis=0)
    conv = ext[5:5 + n_rows] * cw_ref[0:1, :]
    for i in range(1, GDN_CONV):
        conv = conv + ext[5 + i:5 + i + n_rows] * cw_ref[i:i + 1, :]
    carry_ref[...] = ext[n_rows:n_rows + 8]
    act = _silu(conv)

    rows = lax.broadcasted_iota(jnp.int32, (chunk, chunk), 0)
    cols = lax.broadcasted_iota(jnp.int32, (chunk, chunk), 1)
    lower = cols <= rows
    strict = cols < rows
    lower01 = jnp.where(lower, 1.0, 0.0)
    ab = ab_ref[...]
    d = GDN_HEAD_DIM
    heads = range(GDN_HEADS)
    units = [(j, h) for j in range(n_sub) for h in heads]
    unit = lambda j, h: j * GDN_HEADS + h
    rows_of = lambda j: slice(j * chunk, (j + 1) * chunk)
    head_cols = lambda x, base, j, h: x[rows_of(j), base + h * d:base + (h + 1) * d]
    l2n = lambda x: x * lax.rsqrt(jnp.sum(x * x, axis=-1, keepdims=True) + NORM_EPS)
    qn = [l2n(head_cols(act, 0, j, h)) * (d ** -0.5) for j, h in units]
    kn = [l2n(head_cols(act, GDN_WIDTH, j, h)) for j, h in units]
    vh = [head_cols(act, 2 * GDN_WIDTH, j, h) for j, h in units]
    beta = [_sigmoid(ab[rows_of(j), GDN_HEADS + h:GDN_HEADS + h + 1]) for j, h in units]
    g = [-jnp.exp(hp_ref[0:1, h:h + 1]) * _softplus(ab[rows_of(j), h:h + 1] + hp_ref[1:2, h:h + 1])
         for j, h in units]
    gc = [_dot_exact_lhs(lower01, jnp.broadcast_to(x, (chunk, d))) for x in g]
    gi = [x[:, :chunk] if chunk <= d else jnp.broadcast_to(x[:, :1], (chunk, chunk)) for x in gc]
    decay = [jnp.exp(jnp.where(lower, x - x.T, -jnp.inf)) for x in gi]
    kb = [k_ * b_ for k_, b_ in zip(kn, beta)]
    tri = [jnp.where(strict, _dot_nt(kb_, k_) * dc, 0.0) for kb_, k_, dc in zip(kb, kn, decay)]
    attn = [jnp.where(lower, _dot_nt(q_, k_) * dc, 0.0) for q_, k_, dc in zip(qn, kn, decay)]
    t_inv = _inv_i_minus_many([-t for t in tri], chunk, chunk)
    e_gc = [jnp.exp(x) for x in gc]
    uw = [_dot(t_inv[i], jnp.concatenate([vh[i] * beta[i], kb[i] * e_gc[i]], axis=-1)) for i in range(len(units))]
    q_in = [q_ * e_ for q_, e_ in zip(qn, e_gc)]
    g_last = [x[chunk - 1:chunk, :] for x in gc]
    k_out = [k_ * jnp.exp(gl - x) for k_, gl, x in zip(kn, g_last, gc)]
    s = [s_ref[h] for h in heads]
    o = [None] * len(units)
    for j in range(n_sub):
        ids = [unit(j, h) for h in heads]
        v_new = [uw[i][:, :d] - _dot_nt(uw[i][:, d:], s[h]) for h, i in zip(heads, ids)]
        for h, i in zip(heads, ids):
            o[i] = _dot_nt(q_in[i], s[h]) + _dot(attn[i], v_new[h])
        s = [s[h] * jnp.exp(g_last[i]) + _dot_tn(v_new[h], k_out[i]) for h, i in zip(heads, ids)]
    for h in heads:
        s_ref[h] = s[h]
    for j, h in units:
        oi = o[unit(j, h)]
        on = oi * lax.rsqrt(jnp.mean(oi * oi, axis=-1, keepdims=True) + NORM_EPS) * nw_ref[...]
        gate = gate_ref[rows_of(j), h * d:(h + 1) * d]
        o_ref[rows_of(j), h * d:(h + 1) * d] = (on * _silu(gate)).astype(o_ref.dtype)

    @pl.when(c == pl.num_programs(1) - 1)
    def _():
        for h in range(GDN_HEADS):
            s_out_ref[0, h] = s_ref[h].T


def _gdn(y_all, row0, n_seq, seq_len, chunk, n_sub, conv_buf, state0, conv_w, a_log, dt_bias, norm_w):
    step_rows = chunk * n_sub
    n_chunks = seq_len // step_rows
    blk0 = row0 // step_rows
    row_map = lambda col: (lambda s, c: (blk0 + s * n_chunks + c, col))
    cbuf = jnp.pad(conv_buf, ((0, 0), (8 - (GDN_CONV - 1), 0), (0, 0)))
    cw = jnp.pad(conv_w, ((0, 8 - GDN_CONV), (0, 0)))
    head_params = jnp.zeros((8, 128), F32).at[0, :GDN_HEADS].set(a_log).at[1, :GDN_HEADS].set(dt_bias)
    state_spec = pl.BlockSpec((1, GDN_HEADS, GDN_HEAD_DIM, GDN_HEAD_DIM), lambda s, c: (s, 0, 0, 0))
    return pl.pallas_call(
        functools.partial(_gdn_kernel, chunk=chunk, n_sub=n_sub),
        grid=(n_seq, n_chunks),
        in_specs=[pl.BlockSpec((8, 128), lambda s, c: (0, 0)),
                  pl.BlockSpec((step_rows, 3 * GDN_WIDTH), row_map(1)),
                  pl.BlockSpec((step_rows, COL_BLOCK), row_map(6)),
                  pl.BlockSpec((step_rows, COL_BLOCK), row_map(7)),
                  pl.BlockSpec((1, 8, 3 * GDN_WIDTH), lambda s, c: (s, 0, 0)),
                  pl.BlockSpec((8, 3 * GDN_WIDTH), lambda s, c: (0, 0)),
                  pl.BlockSpec((1, GDN_HEAD_DIM), lambda s, c: (0, 0)),
                  state_spec],
        out_specs=[pl.BlockSpec((step_rows, GDN_WIDTH), lambda s, c: (s * n_chunks + c, 0)), state_spec],
        out_shape=[jax.ShapeDtypeStruct((n_seq * seq_len, GDN_WIDTH), BF16),
                   jax.ShapeDtypeStruct(state0.shape, F32)],
        scratch_shapes=[pltpu.VMEM((GDN_HEADS, GDN_HEAD_DIM, GDN_HEAD_DIM), F32),
                        pltpu.VMEM((8, 3 * GDN_WIDTH), F32)],
        compiler_params=_params("parallel", "arbitrary"),
        name="gdn",
    )(head_params, y_all, y_all, y_all, cbuf, cw, norm_w.reshape(1, GDN_HEAD_DIM), state0)


def _rwkv_kernel(r_ref, k_ref, v_ref, lw_ref, a_ref, g_ref, kk_ref, ka_ref, rk_ref, lg_ref, lb_ref,
                 s0_ref, o_ref, s_out_ref, s_ref, *, chunk, group):
    c = pl.program_id(1)
    hd = RWKV_HEAD_DIM
    width = group * hd
    size = group * chunk
    n_groups = D_MODEL // width

    @pl.when(c == 0)
    def _():
        for gi in range(n_groups):
            for hh in range(group):
                blocks = [s0_ref[0, gi * group + hh] if h2 == hh else jnp.zeros((hd, hd), F32)
                          for h2 in range(group)]
                s_ref[gi, hh * hd:(hh + 1) * hd, :] = jnp.concatenate(blocks, axis=-1)
    rows = lax.broadcasted_iota(jnp.int32, (size, size), 0)
    cols = lax.broadcasted_iota(jnp.int32, (size, size), 1)
    same = rows // chunk == cols // chunk
    lower = jnp.logical_and(same, cols <= rows)
    strict = jnp.logical_and(same, cols < rows)
    r2 = lax.broadcasted_iota(jnp.int32, (size, width), 0)
    c2 = lax.broadcasted_iota(jnp.int32, (size, width), 1)
    own = r2 // chunk == c2 // hd
    groups = range(D_MODEL // width)
    cols_of = lambda gi: slice(gi * width, (gi + 1) * width)
    tile = lambda x: jnp.concatenate([x] * group, axis=0)
    stack = lambda x: jnp.where(own, tile(x), 0.0)
    unstack = lambda x: sum(x[hh * chunk:(hh + 1) * chunk] for hh in range(group))

    tr = lax.broadcasted_iota(jnp.int32, (chunk, chunk), 0)
    tc = lax.broadcasted_iota(jnp.int32, (chunk, chunk), 1)
    lw_all = lw_ref[...]
    cw_all = _dot_exact_lhs(jnp.where(tc <= tr, 1.0, 0.0), lw_all)
    tot_all = jnp.sum(lw_all, axis=0, keepdims=True)
    a_all = a_ref[...]
    k_all = k_ref[...]
    v_all = v_ref[...]
    e_neg = jnp.exp(-cw_all)
    e_end = jnp.exp(tot_all - cw_all)
    neg_e_prev = -jnp.exp(cw_all - lw_all)
    r_cw = r_ref[...] * jnp.exp(cw_all)
    k2_all = k_all * (1.0 + (a_all - 1.0) * ka_ref[...])
    kk_raw = k_all * kk_ref[...]
    rk_all = r_ref[...] * k2_all * rk_ref[...]
    kt_all, k_end_all = k2_all * e_neg, k2_all * e_end

    kk = [stack(kk_raw[:, cols_of(gi)]) for gi in groups]
    kk = [x * lax.rsqrt(jnp.sum(x * x, axis=-1, keepdims=True) + NORM_EPS) for x in kk]
    b = [unstack(kk[gi]) * a_all[:, cols_of(gi)] for gi in groups]
    at = [kk[gi] * tile(neg_e_prev[:, cols_of(gi)]) for gi in groups]
    rt = [stack(r_cw[:, cols_of(gi)]) for gi in groups]
    v = [stack(v_all[:, cols_of(gi)]) for gi in groups]
    bt = [tile(b[gi] * e_neg[:, cols_of(gi)]) for gi in groups]
    kt = [tile(kt_all[:, cols_of(gi)]) for gi in groups]
    b_end = [stack(b[gi] * e_end[:, cols_of(gi)]) for gi in groups]
    k_end = [stack(k_end_all[:, cols_of(gi)]) for gi in groups]
    a_ab = [jnp.where(strict, _dot_nt(at[gi], bt[gi]), 0.0) for gi in groups]
    a_ak = [jnp.where(strict, _dot_nt(at[gi], kt[gi]), 0.0) for gi in groups]
    a_rb = [jnp.where(lower, _dot_nt(rt[gi], bt[gi]), 0.0) for gi in groups]
    a_rk = [jnp.where(lower, _dot_nt(rt[gi], kt[gi]), 0.0) for gi in groups]
    t_inv = _inv_i_minus_many(a_ab, size, chunk)
    s = [s_ref[gi] for gi in groups]
    rhs = [_dot_nt(at[gi], s[gi]) + _dot(a_ak[gi], v[gi]) for gi in groups]
    o_past = [_dot_nt(rt[gi], s[gi]) + _dot(a_rk[gi], v[gi]) for gi in groups]
    u = [_dot(t_inv[gi], rhs[gi]) for gi in groups]
    o = [o_past[gi] + _dot(a_rb[gi], u[gi]) for gi in groups]
    for gi in groups:
        s_ref[gi] = (s[gi] * jnp.exp(tot_all[:, cols_of(gi)]) + _dot_tn(u[gi], b_end[gi])
                     + _dot_tn(v[gi], k_end[gi]))
    for gi in groups:
        sl = cols_of(gi)
        mu = jnp.sum(o[gi], axis=-1, keepdims=True) * (1.0 / hd)
        oc = jnp.where(own, o[gi] - mu, 0.0)
        var = jnp.sum(oc * oc, axis=-1, keepdims=True) * (1.0 / hd)
        bonus = jnp.sum(stack(rk_all[:, sl]), axis=-1, keepdims=True) * v[gi]
        out = unstack(oc * lax.rsqrt(var + RWKV_GN_EPS)) * lg_ref[:, sl] + lb_ref[:, sl] + unstack(bonus)
        o_ref[:, sl] = (out * g_ref[:, sl]).astype(o_ref.dtype)

    @pl.when(c == pl.num_programs(1) - 1)
    def _():
        for gi in range(n_groups):
            for hh in range(group):
                s_out_ref[0, gi * group + hh] = s_ref[gi, hh * hd:(hh + 1) * hd, hh * hd:(hh + 1) * hd]


def _rwkv(r, k, v, lw, a, g, row0, n_seq, seq_len, chunk, state0, k_k, k_a, r_k, lnx_g, lnx_b, group=4):
    d = D_MODEL
    n_groups = RWKV_HEADS // group
    width = group * RWKV_HEAD_DIM
    n_chunks = seq_len // chunk
    blk0 = row0 // chunk
    tok = pl.BlockSpec((chunk, d), lambda s, c: (blk0 + s * n_chunks + c, 0))
    vec = pl.BlockSpec((1, d), lambda s, c: (0, 0))
    state_spec = pl.BlockSpec((1, RWKV_HEADS, RWKV_HEAD_DIM, RWKV_HEAD_DIM), lambda s, c: (s, 0, 0, 0))
    return pl.pallas_call(
        functools.partial(_rwkv_kernel, chunk=chunk, group=group),
        grid=(n_seq, n_chunks),
        in_specs=[tok] * 6 + [vec] * 5 + [state_spec],
        out_specs=[pl.BlockSpec((chunk, d), lambda s, c: (s * n_chunks + c, 0)), state_spec],
        out_shape=[jax.ShapeDtypeStruct((n_seq * seq_len, d), BF16),
                   jax.ShapeDtypeStruct(state0.shape, F32)],
        scratch_shapes=[pltpu.VMEM((n_groups, width, width), F32)],
        compiler_params=_params("parallel", "arbitrary"),
        name="rwkv_wkv",
    )(r, k, v, lw, a, g, k_k.reshape(1, d), k_a.reshape(1, d), r_k.reshape(1, d),
      lnx_g.reshape(1, d), lnx_b.reshape(1, d), state0)


def _forward(x_prompt, x_sample, cache_k, cache_v, page_table, state_gdn_conv, state_gdn,
             state_rwkv_shift, state_rwkv, p, *, tm, tm_proj, sb_tq, sb_tk, pages_per_step,
             chunk_prompt, gdn_chunk_sample, rwkv_chunk_sample, ffn_tf, tm_moe, moe_rows, moe_strip, gdn_sub):
    bp, t_p, d = x_prompt.shape
    bs, t_s, _ = x_sample.shape
    assert bp == 1
    n_p, n_s = bp * t_p, bs * t_s
    x = jnp.concatenate([x_prompt.reshape(n_p, d), x_sample.reshape(n_s, d)], axis=0)

    w_in = p['w_in0']
    cut = 3 * SB_WIDTH + 3 * GDN_WIDTH
    w_pad = jnp.concatenate(
        [w_in[:, :cut], w_in[:, cut + 2 * GDN_HEADS:], w_in[:, cut:cut + 2 * GDN_HEADS],
         jnp.zeros((d, COL_BLOCK - 2 * GDN_HEADS), F32)], axis=1).astype(BF16)
    y = _matmul(x, w_pad, tm, COL_BLOCK)
    q = y[:, :SB_WIDTH]
    k_rows = y[:, SB_WIDTH:2 * SB_WIDTH]
    v_rows = y[:, 2 * SB_WIDTH:3 * SB_WIDTH]
    gdn_rows = y[:, 3 * SB_WIDTH:3 * SB_WIDTH + 3 * GDN_WIDTH]

    o_sb_p = _sb_prompt(q[:n_p], k_rows[:n_p], v_rows[:n_p], p['sb_bias'], sb_tq, sb_tk)
    shape_s = (bs, t_s, SB_WIDTH)
    o_sb_s = _sb_sample(q[n_p:].reshape(shape_s), k_rows[n_p:].reshape(shape_s), v_rows[n_p:].reshape(shape_s),
                        cache_k, cache_v, page_table, p['sb_bias'], pages_per_step)
    o_sb = jnp.concatenate([o_sb_p, o_sb_s.reshape(n_s, SB_WIDTH)], axis=0)

    gdn_args = (p['gdn_conv_w'], p['gdn_a_log'], p['gdn_dt_bias'], p['gdn_norm_w'])
    o_gdn_p, gdn_state_p = _gdn(y, 0, bp, t_p, chunk_prompt, gdn_sub,
                                jnp.zeros((bp, GDN_CONV - 1, 3 * GDN_WIDTH), F32),
                                jnp.zeros((bp,) + state_gdn.shape[1:], F32), *gdn_args)
    o_gdn_s, gdn_state_s = _gdn(y, n_p, bs, t_s, gdn_chunk_sample, 1, state_gdn_conv, state_gdn, *gdn_args)
    o_gdn = jnp.concatenate([o_gdn_p, o_gdn_s], axis=0)
    conv_p = gdn_rows[:n_p].reshape(bp, t_p, -1)[:, t_p - (GDN_CONV - 1):]
    conv_s = gdn_rows[n_p:].reshape(bs, t_s, -1)[:, t_s - (GDN_CONV - 1):]

    w_out = p['w_out0'].astype(BF16)
    x = _proj_ln([o_sb, o_gdn], [w_out[:SB_WIDTH], w_out[SB_WIDTH:]], x,
                 p['ln_gamma'][0, 0], p['ln_beta'][0, 0], tm, "out_proj_ln")
    x = _ffn_ln(x, p['ffn_gate'].astype(BF16), p['ffn_up'].astype(BF16), p['ffn_down'].astype(BF16),
                p['ln_gamma'][0, 1], p['ln_beta'][0, 1], tm, ffn_tf)

    x_p = x[:n_p].reshape(bp, t_p, d)
    x_s = x[n_p:].reshape(bs, t_s, d)
    prev_p = jnp.concatenate([jnp.zeros((bp, 1, d), F32), x_p[:, :-1]], axis=1)
    prev_s = jnp.concatenate([state_rwkv_shift[:, None, :], x_s[:, :-1]], axis=1)
    x_prev = jnp.concatenate([prev_p.reshape(n_p, d), prev_s.reshape(n_s, d)], axis=0)
    bf = lambda name: p[name].astype(BF16)
    r, k, v, lw, a, g = _rwkv_proj(
        x, x_prev, p['rwkv_mix'], bf('rwkv_w_r'), bf('rwkv_w_k'), bf('rwkv_w_v'), bf('rwkv_w1'), bf('rwkv_w2'),
        bf('rwkv_a1'), bf('rwkv_a2'), bf('rwkv_g1'), bf('rwkv_g2'),
        p['rwkv_w0'].reshape(1, d), p['rwkv_a0'].reshape(1, d), tm_proj)
    rwkv_args = (p['rwkv_k_k'], p['rwkv_k_a'], p['rwkv_r_k'], p['rwkv_lnx_g'], p['rwkv_lnx_b'])
    o_p, rwkv_state_p = _rwkv(r, k, v, lw, a, g, 0, bp, t_p, chunk_prompt,
                              jnp.zeros((bp,) + state_rwkv.shape[1:], F32), *rwkv_args)
    o_s, rwkv_state_s = _rwkv(r, k, v, lw, a, g, n_p, bs, t_s, rwkv_chunk_sample, state_rwkv, *rwkv_args)
    x1 = _proj_ln([jnp.concatenate([o_p, o_s], axis=0)], [bf('rwkv_w_o')], x,
                  p['ln_gamma'][1, 0], p['ln_beta'][1, 0], tm, "rwkv_out_ln")
    w_router = jnp.pad(p['moe_router'], ((0, 0), (0, 128 - N_EXPERTS)))
    out = _moe_ln(x1, bf('moe_gate'), bf('moe_up'), bf('moe_down'), p['ln_gamma'][1, 1], p['ln_beta'][1, 1],
                  w_router, tm_moe, ffn_tf, moe_rows, moe_strip)

    heads = lambda rows, b_, t_: rows.reshape(b_, t_, SB_HEADS, SB_HEAD_DIM)
    return (out[:n_p].reshape(bp, t_p, d), out[n_p:].reshape(bs, t_s, d),
            heads(k_rows[:n_p], bp, t_p), heads(v_rows[:n_p], bp, t_p), conv_p, gdn_state_p,
            x_p[:, -1], rwkv_state_p,
            heads(k_rows[n_p:], bs, t_s), heads(v_rows[n_p:], bs, t_s), conv_s, gdn_state_s,
            x_s[:, -1], rwkv_state_s)


def kernel(x_prompt, x_sample, cache_k, cache_v, page_table, state_gdn_conv, state_gdn, state_rwkv_shift, state_rwkv, w_in0, sb_bias, gdn_conv_w, gdn_a_log, gdn_dt_bias, gdn_norm_w, w_out0, ffn_gate, ffn_up, ffn_down, rwkv_mix, rwkv_w_r, rwkv_w_k, rwkv_w_v, rwkv_w0, rwkv_w1, rwkv_w2, rwkv_a0, rwkv_a1, rwkv_a2, rwkv_g1, rwkv_g2, rwkv_k_k, rwkv_k_a, rwkv_r_k, rwkv_lnx_g, rwkv_lnx_b, rwkv_w_o, moe_router, moe_gate, moe_up, moe_down, ln_gamma, ln_beta):
    p = dict(w_in0=w_in0, sb_bias=sb_bias, gdn_conv_w=gdn_conv_w, gdn_a_log=gdn_a_log, gdn_dt_bias=gdn_dt_bias,
             gdn_norm_w=gdn_norm_w, w_out0=w_out0, ffn_gate=ffn_gate, ffn_up=ffn_up, ffn_down=ffn_down,
             rwkv_mix=rwkv_mix, rwkv_w_r=rwkv_w_r, rwkv_w_k=rwkv_w_k, rwkv_w_v=rwkv_w_v,
             rwkv_w0=rwkv_w0, rwkv_w1=rwkv_w1, rwkv_w2=rwkv_w2, rwkv_a0=rwkv_a0, rwkv_a1=rwkv_a1,
             rwkv_a2=rwkv_a2, rwkv_g1=rwkv_g1, rwkv_g2=rwkv_g2, rwkv_k_k=rwkv_k_k, rwkv_k_a=rwkv_k_a,
             rwkv_r_k=rwkv_r_k, rwkv_lnx_g=rwkv_lnx_g, rwkv_lnx_b=rwkv_lnx_b, rwkv_w_o=rwkv_w_o,
             moe_router=moe_router, moe_gate=moe_gate, moe_up=moe_up, moe_down=moe_down,
             ln_gamma=ln_gamma, ln_beta=ln_beta)
    return _forward(x_prompt, x_sample, cache_k, cache_v, page_table, state_gdn_conv, state_gdn,
                    state_rwkv_shift, state_rwkv, p, tm=640, tm_proj=320, sb_tq=512, sb_tk=256,
                    pages_per_step=32, chunk_prompt=64, gdn_chunk_sample=8, rwkv_chunk_sample=8, ffn_tf=1408,
                    tm_moe=1280, moe_rows=384, moe_strip=256, gdn_sub=4)
```
